```python
import jax, jax.numpy as jnp
from jax import lax
import numpy as np

D_MODEL = 1024
BATCH = 4
SEQ = 8192
DEPTH = 2
DEC_BATCH = 128
DEC_SEQ = 1
PAST_LEN = 16384
PAGE_SIZE = 128

N_AB_LAYERS = (DEPTH + 1) // 2
N_CD_LAYERS = DEPTH // 2

MLA_HEADS = 8
MLA_NOPE = 64
MLA_ROPE = 32
MLA_V = 64
MLA_Q_RANK = 256
MLA_KV_RANK = 128
MLA_WIDTH = MLA_HEADS * MLA_V
MLA_SCALE = (MLA_NOPE + MLA_ROPE) ** -0.5
ROPE_THETA = 10000.0
Q_BLOCK = 128

GM_GROUPS = 8
GM_WIDTH = 512
GM_GROUP_DIM = GM_WIDTH // GM_GROUPS
GM_CHUNK = 128

SC_WIDTH = 512
CONV_LEN = 3

MOBA_HEADS = 8
MOBA_HEAD_DIM = 64
MOBA_WIDTH = MOBA_HEADS * MOBA_HEAD_DIM
MOBA_BLOCK = 256
MOBA_TOPK = 3
MOBA_Q_CHUNK = 64
MOBA_SCALE = MOBA_HEAD_DIM ** -0.5

AB_SPLITS = (MLA_Q_RANK, MLA_KV_RANK, MLA_ROPE, MLA_WIDTH, GM_WIDTH, GM_WIDTH, GM_WIDTH)
AB_IN = sum(AB_SPLITS)
AB_OUT = MLA_WIDTH + GM_WIDTH
CD_SPLITS = (SC_WIDTH,) * 4 + (MOBA_WIDTH,) * 4
CD_IN = sum(CD_SPLITS)
CD_OUT = SC_WIDTH + MOBA_WIDTH
NORM_EPS = 1e-6

kernel_name = 'hybrid_mla_chunkmlp_shortconv_moba_step'

F32 = jnp.float32


def rmsnorm(x, g):
    xf = x.astype(F32)
    y = xf * lax.rsqrt(jnp.mean(xf * xf, axis=-1, keepdims=True) + NORM_EPS)
    return (y * g.astype(F32)).astype(x.dtype)


def layernorm(x, g, b):
    xf = x.astype(F32)
    mu = jnp.mean(xf, axis=-1, keepdims=True)
    var = jnp.mean(jnp.square(xf - mu), axis=-1, keepdims=True)
    return ((xf - mu) * lax.rsqrt(var + NORM_EPS) * g.astype(F32) + b.astype(F32)).astype(x.dtype)


def split_cols(z, sizes):
    return jnp.split(z, np.cumsum(sizes)[:-1].tolist(), axis=-1)


def apply_rope(x, pos):
    half = MLA_ROPE // 2
    inv = ROPE_THETA ** (-jnp.arange(half, dtype=F32) / half)
    ang = pos.astype(F32)[:, None] * inv[None, :]
    ang = ang.reshape((1, ang.shape[0]) + (1,) * (x.ndim - 3) + (half,))
    cos, sin = jnp.cos(ang), jnp.sin(ang)
    xf = x.astype(F32)
    x1, x2 = xf[..., :half], xf[..., half:]
    return jnp.concatenate([x1 * cos - x2 * sin, x2 * cos + x1 * sin], axis=-1).astype(x.dtype)


def merge_out(a, ga, b, gb, w_out):
    return jnp.concatenate([a * jax.nn.silu(ga), b * jax.nn.silu(gb)], axis=-1) @ w_out


def ab_project(h, pos, w_in, q_norm_g, kv_norm_g, w_uq, ln_g, ln_b):
    cq, ckv, kr, g_mla, u, v, g_gm = split_cols(h @ w_in, AB_SPLITS)
    q = jnp.einsum('bsr,rhe->bshe', rmsnorm(cq, q_norm_g), w_uq)
    q_nope, q_rope = q[..., :MLA_NOPE], apply_rope(q[..., MLA_NOPE:], pos)
    ckv = rmsnorm(ckv, kv_norm_g)
    kr = apply_rope(kr, pos)
    v = layernorm(v, ln_g, ln_b)
    return q_nope, q_rope, ckv, kr, g_mla, u, v, g_gm


def mla_attend_prompt(q_nope, q_rope, ckv, kr, w_ukv):
    B, S = ckv.shape[:2]
    kv = jnp.einsum('bsr,rhe->bshe', ckv, w_ukv)
    k_nope, val = kv[..., :MLA_NOPE], kv[..., MLA_NOPE:]
    kpos = jnp.arange(S)

    def q_block(i):
        start = i * Q_BLOCK
        qn = lax.dynamic_slice_in_dim(q_nope, start, Q_BLOCK, axis=1)
        qr = lax.dynamic_slice_in_dim(q_rope, start, Q_BLOCK, axis=1)
        s = (jnp.einsum('bqhe,bkhe->bhqk', qn, k_nope, preferred_element_type=F32)
             + jnp.einsum('bqhe,bke->bhqk', qr, kr, preferred_element_type=F32)) * MLA_SCALE
        qpos = start + jnp.arange(Q_BLOCK)
        s = jnp.where(kpos[None, :] <= qpos[:, None], s, -jnp.inf)
        p = jax.nn.softmax(s, axis=-1).astype(val.dtype)
        return jnp.einsum('bhqk,bkhe->bqhe', p, val)

    o = lax.map(q_block, jnp.arange(S // Q_BLOCK))
    return jnp.moveaxis(o, 0, 1).reshape(B, S, MLA_HEADS, MLA_V)


def mla_attend_sample(q_nope, q_rope, ckv_new, kr_new, ckv_past, kr_past, w_ukv):
    T = q_nope.shape[1]
    w_uk, w_uv = w_ukv[..., :MLA_NOPE], w_ukv[..., MLA_NOPE:]
    q_lat = jnp.einsum('bshe,rhe->bshr', q_nope, w_uk)

    def scores(c, r):
        return (jnp.einsum('bshr,btr->bhst', q_lat, c, preferred_element_type=F32)
                + jnp.einsum('bshe,bte->bhst', q_rope, r, preferred_element_type=F32)) * MLA_SCALE

    s_new = jnp.where(jnp.tril(jnp.ones((T, T), dtype=bool)), scores(ckv_new, kr_new), -jnp.inf)
    s = jnp.concatenate([scores(ckv_past, kr_past), s_new], axis=-1)
    p = jax.nn.softmax(s, axis=-1).astype(ckv_new.dtype)
    P = ckv_past.shape[1]
    o_lat = (jnp.einsum('bhst,btr->bshr', p[..., :P], ckv_past)
             + jnp.einsum('bhst,btr->bshr', p[..., P:], ckv_new))
    return jnp.einsum('bshr,rhe->bshe', o_lat, w_uv)


def chunk_mix(u, v, ws, bs):
    B, S, W = v.shape
    nc = -(-S // GM_CHUNK)
    vp = jnp.pad(v, ((0, 0), (0, nc * GM_CHUNK - S), (0, 0)))
    vp = vp.reshape(B, nc, GM_CHUNK, GM_GROUPS, GM_GROUP_DIM)
    causal = jnp.tril(jnp.ones((GM_CHUNK, GM_CHUNK), dtype=bool))
    w = jnp.where(causal, ws, 0)
    s = jnp.einsum('gtj,bcjgd->bctgd', w, vp) + bs.T[None, None, :, :, None]
    return u * s.reshape(B, nc * GM_CHUNK, W)[:, :S]


def cd_project(h, w_in):
    sc_h, sc_b, sc_c, sc_g, q, k, v, mo_g = split_cols(h @ w_in, CD_SPLITS)
    B, S = h.shape[:2]
    heads = lambda t: t.reshape(B, S, MOBA_HEADS, MOBA_HEAD_DIM)
    return sc_c * sc_h, sc_b, sc_g, heads(q), heads(k), heads(v), mo_g


def short_conv(pre, buf, w):
    xp = jnp.concatenate([buf.astype(pre.dtype), pre], axis=1)
    y = lax.conv_general_dilated(xp, w[:, None, :].astype(pre.dtype), window_strides=(1,),
                                 padding='VALID', dimension_numbers=('NWC', 'WIO', 'NWC'),
                                 feature_group_count=pre.shape[-1])
    return y, xp[:, -(CONV_LEN - 1):]


def moba_attend_prompt(q, k, v):
    B, S, H, Dh = q.shape
    nb = -(-S // MOBA_BLOCK)
    nsel = max(nb, MOBA_TOPK)
    pad = ((0, 0), (0, nb * MOBA_BLOCK - S), (0, 0), (0, 0))
    to_blocks = lambda t: jnp.pad(t, pad).reshape(B, nb, MOBA_BLOCK, H, Dh).transpose(0, 3, 1, 2, 4)
    kb, vb = to_blocks(k), to_blocks(v)
    k_mean = jnp.mean(kb.astype(F32), axis=3)
    k_mean = jnp.pad(k_mean, ((0, 0), (0, 0), (0, nsel - nb), (0, 0)))
    bi = jnp.arange(B)[:, None, None, None]
    hi = jnp.arange(H)[None, :, None, None]
    blk = jnp.arange(nsel)
    loff = jnp.arange(MOBA_BLOCK)
    n_sel_keys = MOBA_TOPK * MOBA_BLOCK

    def q_chunk(i):
        start = i * MOBA_Q_CHUNK
        qc = jnp.moveaxis(lax.dynamic_slice_in_dim(q, start, MOBA_Q_CHUNK, axis=1), 2, 1)
        cb = start // MOBA_BLOCK
        gate = jnp.einsum('bhqd,bhnd->bhqn', qc.astype(F32), k_mean)
        gate = jnp.where(blk < cb, gate, -jnp.inf)
        _, idx = lax.top_k(gate, MOBA_TOPK)
        sel_ok = idx < cb
        idx = jnp.minimum(idx, nb - 1)
        ks, vs = kb[bi, hi, idx], vb[bi, hi, idx]
        k_own = lax.dynamic_index_in_dim(kb, cb, axis=2, keepdims=False)
        v_own = lax.dynamic_index_in_dim(vb, cb, axis=2, keepdims=False)
        s_sel = jnp.einsum('bhqd,bhqnld->bhqnl', qc, ks, preferred_element_type=F32) * MOBA_SCALE
        s_sel = jnp.where(sel_ok[..., None], s_sel, -jnp.inf).reshape(B, H, MOBA_Q_CHUNK, n_sel_keys)
        qpos = start + jnp.arange(MOBA_Q_CHUNK)
        s_own = jnp.einsum('bhqd,bhld->bhql', qc, k_own, preferred_element_type=F32) * MOBA_SCALE
        s_own = jnp.where(cb * MOBA_BLOCK + loff[None, :] <= qpos[:, None], s_own, -jnp.inf)
        p = jax.nn.softmax(jnp.concatenate([s_sel, s_own], axis=-1), axis=-1).astype(v.dtype)
        o = (jnp.einsum('bhqm,bhqmd->bhqd', p[..., :n_sel_keys],
                        vs.reshape(B, H, MOBA_Q_CHUNK, n_sel_keys, Dh))
             + jnp.einsum('bhql,bhld->bhqd', p[..., n_sel_keys:], v_own))
        return jnp.moveaxis(o, 1, 2)

    o = lax.map(q_chunk, jnp.arange(S // MOBA_Q_CHUNK))
    return jnp.moveaxis(o, 0, 1).reshape(B, S, H * Dh)


def moba_attend_sample(q, k, v, cache_k, cache_v, page_table, li):
    DB, T, H, Dh = q.shape
    n_pages = page_table.shape[1]
    past_len = n_pages * PAGE_SIZE
    ppb = MOBA_BLOCK // PAGE_SIZE
    n_new = -(-T // PAGE_SIZE)
    nb = -(-(n_pages + n_new) // ppb)
    nsel = max(nb, MOBA_TOPK)
    pad_new = ((0, 0), (0, n_new * PAGE_SIZE - T), (0, 0), (0, 0))
    k_new = jnp.pad(k, pad_new).reshape(DB, n_new, PAGE_SIZE, H, Dh)
    v_new = jnp.pad(v, pad_new).reshape(DB, n_new, PAGE_SIZE, H, Dh)
    past_sums = lax.map(lambda pt: jnp.sum(cache_k[li, pt].astype(F32), axis=1), page_table)
    page_sums = jnp.concatenate([past_sums, jnp.sum(k_new.astype(F32), axis=2),
                                 jnp.zeros((DB, nb * ppb - n_pages - n_new, H, Dh), F32)], axis=1)
    k_mean = page_sums.reshape(DB, nb, ppb, H, Dh).sum(axis=2) / MOBA_BLOCK
    k_mean = jnp.pad(k_mean, ((0, 0), (0, nsel - nb), (0, 0), (0, 0)))
    qpos = past_len + jnp.arange(T)
    cb = (qpos // MOBA_BLOCK)[None, :, None, None]
    gate = jnp.einsum('bshd,bnhd->bshn', q.astype(F32), k_mean)
    gate = jnp.where(jnp.arange(nsel) < cb, gate, -jnp.inf)
    _, idx = lax.top_k(gate, MOBA_TOPK)
    slot_ok = jnp.concatenate([idx < cb, jnp.ones((DB, T, H, 1), dtype=bool)], axis=-1)
    blocks = jnp.concatenate([jnp.minimum(idx, nb - 1),
                              jnp.broadcast_to(cb, (DB, T, H, 1))], axis=-1)
    lpage = blocks[..., None] * ppb + jnp.arange(ppb)
    in_past = (lpage < n_pages)[..., None, None]
    phys = page_table[jnp.arange(DB)[:, None, None, None, None], jnp.minimum(lpage, n_pages - 1)]
    newp = jnp.clip(lpage - n_pages, 0, n_new - 1)
    b6 = jnp.arange(DB)[:, None, None, None, None, None]
    h6 = jnp.arange(H)[None, None, :, None, None, None]
    off = jnp.arange(PAGE_SIZE)
    n_keys = (MOBA_TOPK + 1) * MOBA_BLOCK

    def gather(cache, new):
        rows = jnp.where(in_past, cache[li, phys[..., None], off, h6], new[b6, newp[..., None], off, h6])
        return rows.reshape(DB, T, H, n_keys, Dh)

    kk, vv = gather(cache_k, k_new), gather(cache_v, v_new)
    kpos = (lpage[..., None] * PAGE_SIZE + off).reshape(DB, T, H, n_keys)
    ok = jnp.repeat(slot_ok, MOBA_BLOCK, axis=-1) & (kpos <= qpos[None, :, None, None])
    s = jnp.einsum('bshd,bshmd->bshm', q, kk, preferred_element_type=F32) * MOBA_SCALE
    p = jax.nn.softmax(jnp.where(ok, s, -jnp.inf), axis=-1).astype(v.dtype)
    return jnp.einsum('bshm,bshmd->bshd', p, vv).reshape(DB, T, H * Dh)


def setup_inputs(seed: int = 0) -> dict:
    key = jax.random.key(seed)
    ks = iter(jax.random.split(key, 40))
    nrm = lambda shape, scale=1.0: jax.random.normal(next(ks), shape, F32) * scale
    gain = lambda shape: 1.0 + nrm(shape, 0.05)
    n_pages = PAST_LEN // PAGE_SIZE
    n_used = DEC_BATCH * n_pages
    n_pool = n_used + max(1, n_used // 4)
    page_table = jax.random.permutation(next(ks), n_pool)[:n_used].reshape(DEC_BATCH, n_pages).astype(jnp.int32)
    return {
        'x_prompt': nrm((BATCH, SEQ, D_MODEL)),
        'x_sample': nrm((DEC_BATCH, DEC_SEQ, D_MODEL)),
        'cache_mla_ckv': nrm((N_AB_LAYERS, n_pool, PAGE_SIZE, MLA_KV_RANK)),
        'cache_mla_krope': nrm((N_AB_LAYERS, n_pool, PAGE_SIZE, MLA_ROPE)),
        'cache_moba_k': nrm((N_CD_LAYERS, n_pool, PAGE_SIZE, MOBA_HEADS, MOBA_HEAD_DIM)),
        'cache_moba_v': nrm((N_CD_LAYERS, n_pool, PAGE_SIZE, MOBA_HEADS, MOBA_HEAD_DIM)),
        'state_conv': nrm((N_CD_LAYERS, DEC_BATCH, CONV_LEN - 1, SC_WIDTH)),
        'page_table': page_table,
        'ab_norm_g': gain((N_AB_LAYERS, D_MODEL)),
        'ab_w_in': nrm((N_AB_LAYERS, D_MODEL, AB_IN), D_MODEL ** -0.5),
        'ab_q_norm_g': gain((N_AB_LAYERS, MLA_Q_RANK)),
        'ab_kv_norm_g': gain((N_AB_LAYERS, MLA_KV_RANK)),
        'ab_w_uq': nrm((N_AB_LAYERS, MLA_Q_RANK, MLA_HEADS, MLA_NOPE + MLA_ROPE), MLA_Q_RANK ** -0.5),
        'ab_w_ukv': nrm((N_AB_LAYERS, MLA_KV_RANK, MLA_HEADS, MLA_NOPE + MLA_V), MLA_KV_RANK ** -0.5),
        'ab_gm_ln_g': gain((N_AB_LAYERS, GM_WIDTH)),
        'ab_gm_ln_b': nrm((N_AB_LAYERS, GM_WIDTH), 0.02),
        'ab_gm_ws': nrm((N_AB_LAYERS, GM_GROUPS, GM_CHUNK, GM_CHUNK), GM_CHUNK ** -0.5),
        'ab_gm_bs': 1.0 + nrm((N_AB_LAYERS, GM_GROUPS, GM_CHUNK), 0.1),
        'ab_w_out': nrm((N_AB_LAYERS, AB_OUT, D_MODEL), AB_OUT ** -0.5),
        'cd_norm_g': gain((N_CD_LAYERS, D_MODEL)),
        'cd_w_in': nrm((N_CD_LAYERS, D_MODEL, CD_IN), D_MODEL ** -0.5),
        'cd_conv_w': nrm((N_CD_LAYERS, CONV_LEN, SC_WIDTH), CONV_LEN ** -0.5),
        'cd_w_out': nrm((N_CD_LAYERS, CD_OUT, D_MODEL), CD_OUT ** -0.5),
        'final_norm_g': gain((D_MODEL,)),
    }


def reference(x_prompt, x_sample, cache_mla_ckv, cache_mla_krope, cache_moba_k, cache_moba_v,
              state_conv, page_table, ab_norm_g, ab_w_in, ab_q_norm_g, ab_kv_norm_g, ab_w_uq,
              ab_w_ukv, ab_gm_ln_g, ab_gm_ln_b, ab_gm_ws, ab_gm_bs, ab_w_out, cd_norm_g, cd_w_in,
              cd_conv_w, cd_w_out, final_norm_g):
    n_pages = page_table.shape[1]
    past_len = n_pages * PAGE_SIZE
    B, S = x_prompt.shape[:2]
    DB, T = x_sample.shape[:2]
    pos_p = jnp.arange(S, dtype=jnp.int32)
    pos_s = past_len + jnp.arange(T, dtype=jnp.int32)

    def ab_in(h, pos, li):
        return ab_project(rmsnorm(h, ab_norm_g[li]), pos, ab_w_in[li], ab_q_norm_g[li],
                          ab_kv_norm_g[li], ab_w_uq[li], ab_gm_ln_g[li], ab_gm_ln_b[li])

    def ab_out(att, g_mla, u, v, g_gm, li):
        gm = chunk_mix(u, v, ab_gm_ws[li], ab_gm_bs[li])
        a = att.reshape(att.shape[0], att.shape[1], MLA_WIDTH)
        return merge_out(a, g_mla, gm, g_gm, ab_w_out[li])

    hp, hs = x_prompt, x_sample
    mla_ckv_p, mla_kr_p, mla_ckv_s, mla_kr_s, chunk_v_s = [], [], [], [], []
    conv_p, conv_s, mk_p, mv_p, mk_s, mv_s = [], [], [], [], [], []
    for layer in range(DEPTH):
        li = layer // 2
        if layer % 2 == 0:
            qn, qr, ckv, kr, g1, u, v, g2 = ab_in(hp, pos_p, li)
            att = mla_attend_prompt(qn, qr, ckv, kr, ab_w_ukv[li])
            hp = hp + ab_out(att, g1, u, v, g2, li)
            mla_ckv_p.append(ckv)
            mla_kr_p.append(kr)
            qn, qr, ckv, kr, g1, u, v, g2 = ab_in(hs, pos_s, li)
            ckv_past = cache_mla_ckv[li, page_table].reshape(DB, past_len, MLA_KV_RANK)
            kr_past = cache_mla_krope[li, page_table].reshape(DB, past_len, MLA_ROPE)
            att = mla_attend_sample(qn, qr, ckv, kr, ckv_past, kr_past, ab_w_ukv[li])
            hs = hs + ab_out(att, g1, u, v, g2, li)
            mla_ckv_s.append(ckv)
            mla_kr_s.append(kr)
            chunk_v_s.append(v)
        else:
            pre, sb, sg, q, k, v, mg = cd_project(rmsnorm(hp, cd_norm_g[li]), cd_w_in[li])
            conv, buf = short_conv(pre, jnp.zeros((B, CONV_LEN - 1, SC_WIDTH), pre.dtype), cd_conv_w[li])
            att = moba_attend_prompt(q, k, v)
            hp = hp + merge_out(sb * conv, sg, att, mg, cd_w_out[li])
            conv_p.append(buf)
            mk_p.append(k)
            mv_p.append(v)
            pre, sb, sg, q, k, v, mg = cd_project(rmsnorm(hs, cd_norm_g[li]), cd_w_in[li])
            conv, buf = short_conv(pre, state_conv[li], cd_conv_w[li])
            att = moba_attend_sample(q, k, v, cache_moba_k, cache_moba_v, page_table, li)
            hs = hs + merge_out(sb * conv, sg, att, mg, cd_w_out[li])
            conv_s.append(buf)
            mk_s.append(k)
            mv_s.append(v)

    y_prompt = rmsnorm(hp, final_norm_g)
    y_sample = rmsnorm(hs, final_norm_g)
    return (y_prompt, y_sample, jnp.stack(mla_ckv_p), jnp.stack(mla_kr_p), jnp.stack(mla_ckv_s),
            jnp.stack(mla_kr_s), jnp.stack(chunk_v_s), jnp.stack(conv_p), jnp.stack(conv_s),
            jnp.stack(mk_p), jnp.stack(mv_p), jnp.stack(mk_s), jnp.stack(mv_s))
```

```python
import functools

import jax
import jax.numpy as jnp
import numpy as np
from jax import lax
from jax.experimental import pallas as pl
from jax.experimental.pallas import tpu as pltpu

F32 = jnp.float32
BF16 = jnp.bfloat16

MLA_HEADS = 8
MLA_NOPE = 64
MLA_ROPE = 32
MLA_V = 64
MLA_Q_RANK = 256
MLA_KV_RANK = 128
MLA_WIDTH = MLA_HEADS * MLA_V
MLA_SCALE = (MLA_NOPE + MLA_ROPE) ** -0.5
ROPE_THETA = 10000.0
GM_GROUPS = 8
GM_WIDTH = 512
GM_GROUP_DIM = GM_WIDTH // GM_GROUPS
GM_CHUNK = 128
SC_WIDTH = 512
CONV_LEN = 3
MOBA_HEADS = 8
MOBA_HEAD_DIM = 64
MOBA_WIDTH = MOBA_HEADS * MOBA_HEAD_DIM
MOBA_BLOCK = 256
MOBA_TOPK = 3
MOBA_SCALE = MOBA_HEAD_DIM ** -0.5
NORM_EPS = 1e-6

LANES = 128
SUBLANES = 8
HEAD_PAD = 128
ROLL_SPARE_TO_ROPE = LANES - MLA_ROPE
MASKED = -1e30
VMEM_LIMIT = 56 * 1024 * 1024


def _dot(a, b):
    return jnp.dot(a, b, preferred_element_type=F32)


def _dot_nt(a, b, precision=None):
    return lax.dot_general(a, b, (((1,), (1,)), ((), ())), precision=precision,
                           preferred_element_type=F32)


def _rms(x, g):
    return x * lax.rsqrt(jnp.mean(x * x, axis=-1, keepdims=True) + NORM_EPS) * g


def _layernorm(x, g, b):
    mu = jnp.mean(x, axis=-1, keepdims=True)
    d = x - mu
    var = jnp.mean(d * d, axis=-1, keepdims=True)
    return d * lax.rsqrt(var + NORM_EPS) * g + b


def _silu(x):
    return x * jax.nn.sigmoid(x)


def _rope_mix(raw, c_keep, c_rot):
    return raw * c_keep + pltpu.roll(raw, ROLL_SPARE_TO_ROPE, 1) * c_rot


def _const_spec(shape):
    nd = len(shape)
    return pl.BlockSpec(shape, lambda *_: (0,) * nd)


def _params(sem):
    return pltpu.CompilerParams(dimension_semantics=sem, vmem_limit_bytes=VMEM_LIMIT)


def _rot_half_cols(w):
    half = w.shape[-1] // 2
    return jnp.concatenate([-w[..., half:], w[..., :half]], axis=-1)


def _prep_ab_weights(w_in, w_uq, w_ukv):
    d = w_in.shape[0]
    o = np.cumsum((0, MLA_Q_RANK, MLA_KV_RANK, MLA_ROPE))
    w_cq, w_ckv, w_kr = (w_in[:, o[i]:o[i + 1]] for i in range(3))
    w_kr_blk = jnp.concatenate(
        [jnp.zeros((d, MLA_NOPE), w_in.dtype), w_kr, _rot_half_cols(w_kr)], axis=1)
    w_a = jnp.concatenate([w_cq, w_ckv, w_kr_blk], axis=1).astype(BF16)
    w_b = w_in[:, o[3]:].astype(BF16)
    uq_nope, uq_rope = w_uq[..., :MLA_NOPE], w_uq[..., MLA_NOPE:]
    w_uq_pad = jnp.concatenate([uq_nope, uq_rope, _rot_half_cols(uq_rope)], axis=-1)
    w_uq_pad = w_uq_pad.reshape(MLA_Q_RANK, MLA_HEADS * HEAD_PAD).astype(BF16)
    w_uk, w_uv = w_ukv[..., :MLA_NOPE], w_ukv[..., MLA_NOPE:]
    w_uk_pad = jnp.concatenate([w_uk, jnp.zeros_like(w_uk)], axis=-1)
    w_uk_pad = w_uk_pad.reshape(MLA_KV_RANK, MLA_HEADS * HEAD_PAD).astype(BF16)
    w_uv_flat = w_uv.reshape(MLA_KV_RANK, MLA_WIDTH).astype(BF16)
    return w_a, w_b, w_uq_pad, w_uk_pad, w_uv_flat


def _rope_tables(pos):
    half = MLA_ROPE // 2
    inv = ROPE_THETA ** (-jnp.arange(half, dtype=F32) / half)
    ang = pos.astype(F32)[:, None] * inv[None, :]
    cos, sin = jnp.cos(ang), jnp.sin(ang)
    n = pos.shape[0]
    cos2 = jnp.concatenate([cos, cos], axis=1)
    sin2 = jnp.concatenate([sin, sin], axis=1)
    zeros_nope = jnp.zeros((n, MLA_NOPE), F32)
    ones_nope = jnp.ones((n, MLA_NOPE), F32)
    spare = jnp.zeros((n, HEAD_PAD - MLA_NOPE - MLA_ROPE), F32)
    q_keep = jnp.concatenate([ones_nope, cos2, spare], axis=1) * MLA_SCALE
    q_rot = jnp.concatenate([zeros_nope, sin2, spare], axis=1) * MLA_SCALE
    k_keep = jnp.concatenate([zeros_nope, cos2, spare], axis=1)
    k_rot = jnp.concatenate([zeros_nope, sin2, spare], axis=1)
    return jnp.stack([q_keep, q_rot, k_keep, k_rot])


def _ab_front(x, c, g, wa, wb, qg, kvg, wuq):
    h = _rms(x, g).astype(BF16)
    za = _dot(h, wa)
    zb = _dot(h, wb)
    cqn = _rms(za[:, :MLA_Q_RANK], qg).astype(BF16)
    q_raw = _dot(cqn, wuq)
    ckv = _rms(za[:, MLA_Q_RANK:MLA_Q_RANK + MLA_KV_RANK], kvg)
    kr128 = _rope_mix(za[:, MLA_Q_RANK + MLA_KV_RANK:], c[2], c[3])
    return q_raw, ckv, kr128, zb


def _ab_in_kernel(x_ref, c_ref, g_ref, wa_ref, wb_ref, qg_ref, kvg_ref, wuq_ref, wuk_ref, wuv_ref,
                  lng_ref, lnb_ref, ws_ref, bse_ref,
                  q_out, k_out, v_out, ckv_out, kr_out, ga_out, gmb_out):
    tm = x_ref.shape[1]
    c = c_ref[...]
    q_raw, ckv, kr128, zb = _ab_front(x_ref[0], c, g_ref[...], wa_ref[...], wb_ref[...],
                                      qg_ref[...], kvg_ref[...], wuq_ref[...])
    ckv_out[0] = ckv
    kr_out[0] = kr128[:, MLA_NOPE:MLA_NOPE + MLA_ROPE]
    ckv_b = ckv.astype(BF16)
    kn = _dot(ckv_b, wuk_ref[...])
    v_out[0] = _dot(ckv_b, wuv_ref[...]).astype(BF16)
    for h in range(MLA_HEADS):
        sl = slice(h * HEAD_PAD, (h + 1) * HEAD_PAD)
        q_out[0, :, sl] = _rope_mix(q_raw[:, sl], c[0], c[1]).astype(BF16)
        k_out[0, :, sl] = (kn[:, sl] + kr128).astype(BF16)
    w = GM_WIDTH
    ga_out[0] = _silu(zb[:, :w]).astype(BF16)
    u, v, g_gm = zb[:, w:2 * w], zb[:, 2 * w:3 * w], zb[:, 3 * w:]
    v_ln = _layernorm(v, lng_ref[...], lnb_ref[...]).astype(BF16)
    row = lax.broadcasted_iota(jnp.int32, (GM_CHUNK, GM_CHUNK), 0)
    col = lax.broadcasted_iota(jnp.int32, (GM_CHUNK, GM_CHUNK), 1)
    lane = lax.broadcasted_iota(jnp.int32, (GM_CHUNK, LANES), 1)
    ws_tril = [jnp.where(col <= row, ws_ref[gi], 0.0).astype(BF16) for gi in range(GM_GROUPS)]
    groups_per_vreg = LANES // GM_GROUP_DIM
    assert groups_per_vreg == 2
    for ci in range(tm // GM_CHUNK):
        rows = slice(ci * GM_CHUNK, (ci + 1) * GM_CHUNK)
        for j in range(w // LANES):
            cols = slice(j * LANES, (j + 1) * LANES)
            vb = v_ln[rows, cols]
            s_lo = _dot(ws_tril[2 * j], vb)
            s_hi = _dot(ws_tril[2 * j + 1], vb)
            s = jnp.where(lane < GM_GROUP_DIM, s_lo, s_hi) + bse_ref[:, cols]
            gm = u[rows, cols] * s * _silu(g_gm[rows, cols])
            gmb_out[0, rows, cols] = gm.astype(BF16)


def _ab_in_prompt(x, tables, norm_g, w_a, w_b, qg, kvg, w_uq_pad, w_uk_pad, w_uv_flat,
                  ln_g, ln_b, ws, bs_exp, tm):
    B, S, D = x.shape
    grid = (S // tm, B)
    tok = lambda w: pl.BlockSpec((1, tm, w), lambda i, b: (b, i, 0))
    in_specs = [
        tok(D),
        pl.BlockSpec((4, tm, LANES), lambda i, b: (0, i, 0)),
        _const_spec((1, D)), _const_spec(w_a.shape), _const_spec(w_b.shape),
        _const_spec((1, MLA_Q_RANK)), _const_spec((1, MLA_KV_RANK)),
        _const_spec(w_uq_pad.shape), _const_spec(w_uk_pad.shape), _const_spec(w_uv_flat.shape),
        _const_spec((1, GM_WIDTH)), _const_spec((1, GM_WIDTH)),
        _const_spec(ws.shape), _const_spec(bs_exp.shape),
    ]
    hp = MLA_HEADS * HEAD_PAD
    out_shape = [
        jax.ShapeDtypeStruct((B, S, hp), BF16), jax.ShapeDtypeStruct((B, S, hp), BF16),
        jax.ShapeDtypeStruct((B, S, MLA_WIDTH), BF16),
        jax.ShapeDtypeStruct((B, S, MLA_KV_RANK), F32), jax.ShapeDtypeStruct((B, S, MLA_ROPE), F32),
        jax.ShapeDtypeStruct((B, S, MLA_WIDTH), BF16), jax.ShapeDtypeStruct((B, S, GM_WIDTH), BF16),
    ]
    out_specs = [tok(hp), tok(hp), tok(MLA_WIDTH), tok(MLA_KV_RANK), tok(MLA_ROPE),
                 tok(MLA_WIDTH), tok(GM_WIDTH)]
    return pl.pallas_call(
        _ab_in_kernel, grid=grid, in_specs=in_specs, out_specs=out_specs, out_shape=out_shape,
        compiler_params=_params(("arbitrary", "arbitrary")), name="ab_in_prompt",
    )(x, tables, norm_g, w_a, w_b, qg, kvg, w_uq_pad, w_uk_pad, w_uv_flat, ln_g, ln_b, ws, bs_exp)


def _tri_tables(n):
    qi, ki = np.tril_indices(n)
    return jnp.asarray(qi, jnp.int32), jnp.asarray(ki, jnp.int32)


def _softmax_step(s, v_pair, h, m_ref, l_ref, acc_ref):
    tk = s.shape[1]
    m_prev, l_prev = m_ref[h], l_ref[h]
    m_next = jnp.maximum(m_prev, jnp.max(s, axis=1, keepdims=True))
    p = jnp.exp(s - jnp.concatenate([m_next] * (tk // LANES), axis=1))
    alpha = jnp.exp(m_prev - m_next)
    l_ref[h] = alpha * l_prev + jnp.sum(p, axis=1, keepdims=True)
    m_ref[h] = m_next
    acc_ref[h] = acc_ref[h] * alpha + _dot(p.astype(BF16), v_pair)


def _attn_init(m_ref, l_ref, acc_ref):
    m_ref[...] = jnp.full(m_ref.shape, MASKED, F32)
    l_ref[...] = jnp.zeros(l_ref.shape, F32)
    acc_ref[...] = jnp.zeros(acc_ref.shape, F32)


def _attn_finish(gate_ref, o_ref, l_ref, acc_ref, n_heads):
    tq = acc_ref.shape[1]
    lane = lax.broadcasted_iota(jnp.int32, (tq, LANES), 1)
    for j in range(n_heads // 2):
        lo = acc_ref[2 * j] / l_ref[2 * j]
        hi = acc_ref[2 * j + 1] / l_ref[2 * j + 1]
        cols = slice(j * LANES, (j + 1) * LANES)
        o = jnp.where(lane < LANES // 2, lo, hi) * gate_ref[0, :, cols].astype(F32)
        o_ref[0, :, cols] = o.astype(o_ref.dtype)


def _mla_attn_kernel(qi_ref, ki_ref, q_ref, k_ref, v_ref, ga_ref, o_ref, m_ref, l_ref, acc_ref):
    t = q_ref.shape[1]
    pair = pl.program_id(1)
    qi, ki = qi_ref[pair], ki_ref[pair]

    @pl.when(ki == 0)
    def _():
        _attn_init(m_ref, l_ref, acc_ref)

    def step(diagonal):
        if diagonal:
            row = lax.broadcasted_iota(jnp.int32, (t, t), 0)
            col = lax.broadcasted_iota(jnp.int32, (t, t), 1)
            keep = col <= row
        for h in range(MLA_HEADS):
            sl = slice(h * HEAD_PAD, (h + 1) * HEAD_PAD)
            s = _dot_nt(q_ref[0, :, sl], k_ref[0, :, sl])
            if diagonal:
                s = jnp.where(keep, s, MASKED)
            vp = v_ref[0, :, (h // 2) * LANES:(h // 2 + 1) * LANES]
            _softmax_step(s, vp, h, m_ref, l_ref, acc_ref)

    @pl.when(ki < qi)
    def _():
        step(False)

    @pl.when(ki == qi)
    def _():
        step(True)
        _attn_finish(ga_ref, o_ref, l_ref, acc_ref, MLA_HEADS)


def _mla_attn_prompt(q, k, v, ga, t):
    B, S, _ = q.shape
    n = S // t
    qi_tbl, ki_tbl = _tri_tables(n)
    qmap = lambda b, p, qi, ki: (b, qi[p], 0)
    kmap = lambda b, p, qi, ki: (b, ki[p], 0)
    hp = MLA_HEADS * HEAD_PAD
    grid_spec = pltpu.PrefetchScalarGridSpec(
        num_scalar_prefetch=2, grid=(B, qi_tbl.shape[0]),
        in_specs=[pl.BlockSpec((1, t, hp), qmap), pl.BlockSpec((1, t, hp), kmap),
                  pl.BlockSpec((1, t, MLA_WIDTH), kmap), pl.BlockSpec((1, t, MLA_WIDTH), qmap)],
        out_specs=pl.BlockSpec((1, t, MLA_WIDTH), qmap),
        scratch_shapes=[pltpu.VMEM((MLA_HEADS, t, LANES), F32), pltpu.VMEM((MLA_HEADS, t, LANES), F32),
                        pltpu.VMEM((MLA_HEADS, t, LANES), F32)])
    return pl.pallas_call(
        _mla_attn_kernel, grid_spec=grid_spec, out_shape=jax.ShapeDtypeStruct((B, S, MLA_WIDTH), BF16),
        compiler_params=_params(("arbitrary", "arbitrary")), name="mla_attn_prompt",
    )(qi_tbl, ki_tbl, q, k, v, ga)


def _cd_front(hp, g2, w2):
    h2 = _rms(hp, g2).astype(BF16)
    return _dot(h2, w2)


def _mid_kernel(a_ref, gm_ref, x_ref, wo_ref, g2_ref, w2_ref, cw_ref,
                hp_out, cg_out, q_out, kb_out, vb_out, mg_out, kf_out, vf_out, km_out, cp_out,
                pre_ref):
    tm = x_ref.shape[1]
    i = pl.program_id(1)
    w = SC_WIDTH
    half = wo_ref.shape[0] // 2
    hp = x_ref[0] + _dot(a_ref[0], wo_ref[:half, :]) + _dot(gm_ref[0], wo_ref[half:, :])
    hp_out[0] = hp
    z = _cd_front(hp, g2_ref[...], w2_ref[...])
    pre = z[:, 2 * w:3 * w] * z[:, :w]

    @pl.when(i == 0)
    def _():
        pre_ref[:SUBLANES, :] = jnp.zeros((SUBLANES, w), F32)

    pre_ref[SUBLANES:, :] = pre
    cw = cw_ref[...]
    conv = (cw[0:1] * pre_ref[SUBLANES - 2:SUBLANES - 2 + tm, :]
            + cw[1:2] * pre_ref[SUBLANES - 1:SUBLANES - 1 + tm, :] + cw[2:3] * pre)
    tail = pre_ref[tm:tm + SUBLANES, :]
    pre_ref[:SUBLANES, :] = tail
    cp_out[0] = tail[SUBLANES - (CONV_LEN - 1):, :]
    cg_out[0] = (z[:, w:2 * w] * conv * _silu(z[:, 3 * w:4 * w])).astype(BF16)
    q_out[0] = (z[:, 4 * w:5 * w] * MOBA_SCALE).astype(BF16)
    k = z[:, 5 * w:6 * w]
    v = z[:, 6 * w:7 * w]
    kf_out[0] = k
    vf_out[0] = v
    kb_out[0] = k.astype(BF16)
    vb_out[0] = v.astype(BF16)
    mg_out[0] = _silu(z[:, 7 * w:]).astype(BF16)
    for j in range(tm // MOBA_BLOCK):
        blk = k[j * MOBA_BLOCK:(j + 1) * MOBA_BLOCK]
        km_out[0, j] = jnp.sum(blk, axis=0, keepdims=True) * (1.0 / MOBA_BLOCK)


def _mid_prompt(a, gm, x, w_out, g2, w2, conv_w, tm):
    B, S, D = x.shape
    w = SC_WIDTH
    nb = S // MOBA_BLOCK
    tok = lambda width: pl.BlockSpec((1, tm, width), lambda b, i: (b, i, 0))
    in_specs = [tok(a.shape[2]), tok(gm.shape[2]), tok(D),
                _const_spec(w_out.shape), _const_spec((1, D)), _const_spec(w2.shape),
                _const_spec(conv_w.shape)]
    bf = lambda: jax.ShapeDtypeStruct((B, S, w), BF16)
    ff = lambda: jax.ShapeDtypeStruct((B, S, w), F32)
    out_shape = [jax.ShapeDtypeStruct((B, S, D), F32), bf(), bf(), bf(), bf(), bf(), ff(), ff(),
                 jax.ShapeDtypeStruct((B, nb, 1, w), F32),
                 jax.ShapeDtypeStruct((B, CONV_LEN - 1, w), F32)]
    bpt = tm // MOBA_BLOCK
    out_specs = [tok(D)] + [tok(w)] * 7 + [
        pl.BlockSpec((1, bpt, 1, w), lambda b, i: (b, i, 0, 0)),
        pl.BlockSpec((1, CONV_LEN - 1, w), lambda b, i: (b, 0, 0))]
    return pl.pallas_call(
        _mid_kernel, grid=(B, S // tm), in_specs=in_specs, out_specs=out_specs, out_shape=out_shape,
        scratch_shapes=[pltpu.VMEM((tm + SUBLANES, w), F32)],
        compiler_params=_params(("arbitrary", "arbitrary")), name="mid_prompt",
    )(a, gm, x, w_out, g2, w2, conv_w)


def _select_top_blocks(gate, n_valid):
    lane = lax.broadcasted_iota(jnp.int32, gate.shape, 1)
    g = jnp.where(lane < n_valid, gate, -jnp.inf)
    sel = jnp.zeros(gate.shape, F32)
    for _ in range(MOBA_TOPK):
        mx = jnp.max(g, axis=1, keepdims=True)
        first = jnp.min(jnp.where(g == mx, lane, LANES), axis=1, keepdims=True)
        pick = (lane == first) & (mx > -jnp.inf)
        sel = jnp.where(pick, 1.0, sel)
        g = jnp.where(pick, -jnp.inf, g)
    return sel


def _moba_attn_kernel(qi_ref, ki_ref, q_ref, k_ref, v_ref, km_ref, mg_ref, o_ref,
                      qm_ref, sel_ref, m_ref, l_ref, acc_ref):
    t = q_ref.shape[1]
    pair = pl.program_id(1)
    qi, ki = qi_ref[pair], ki_ref[pair]
    lane = lax.broadcasted_iota(jnp.int32, (t, LANES), 1)

    @pl.when(ki == 0)
    def _():
        _attn_init(m_ref, l_ref, acc_ref)
        for h in range(MOBA_HEADS):
            cols = slice((h // 2) * LANES, (h // 2 + 1) * LANES)
            in_head = (lane >= LANES // 2) if h % 2 else (lane < LANES // 2)
            qm = jnp.where(in_head, q_ref[0, :, cols].astype(F32), 0.0)
            qm_ref[h] = qm.astype(BF16)
            gate = _dot_nt(qm, km_ref[0, :, cols], precision=lax.Precision.HIGHEST)
            sel_ref[h] = _select_top_blocks(gate, qi)

    def step(diagonal):
        if diagonal:
            row = lax.broadcasted_iota(jnp.int32, (t, t), 0)
            col = lax.broadcasted_iota(jnp.int32, (t, t), 1)
            keep = col <= row
        for h in range(MOBA_HEADS):
            cols = slice((h // 2) * LANES, (h // 2 + 1) * LANES)
            s = _dot_nt(qm_ref[h], k_ref[0, :, cols])
            if diagonal:
                s = jnp.where(keep, s, MASKED)
            else:
                chosen = jnp.sum(jnp.where(lane == ki, sel_ref[h], 0.0), axis=1, keepdims=True)
                s = jnp.where(chosen > 0.0, s, MASKED)
            _softmax_step(s, v_ref[0, :, cols], h, m_ref, l_ref, acc_ref)

    @pl.when(ki < qi)
    def _():
        step(False)

    @pl.when(ki == qi)
    def _():
        step(True)
        _attn_finish(mg_ref, o_ref, l_ref, acc_ref, MOBA_HEADS)


def _moba_attn_prompt(q, k, v, kmean_pad, mg):
    B, S, W = q.shape
    t = MOBA_BLOCK
    qi_tbl, ki_tbl = _tri_tables(S // t)
    qmap = lambda b, p, qi, ki: (b, qi[p], 0)
    kmap = lambda b, p, qi, ki: (b, ki[p], 0)
    grid_spec = pltpu.PrefetchScalarGridSpec(
        num_scalar_prefetch=2, grid=(B, qi_tbl.shape[0]),
        in_specs=[pl.BlockSpec((1, t, W), qmap), pl.BlockSpec((1, t, W), kmap),
                  pl.BlockSpec((1, t, W), kmap),
                  pl.BlockSpec((1, LANES, W), lambda b, p, qi, ki: (b, 0, 0)),
                  pl.BlockSpec((1, t, W), qmap)],
        out_specs=pl.BlockSpec((1, t, W), qmap),
        scratch_shapes=[pltpu.VMEM((MOBA_HEADS, t, LANES), BF16)]
        + [pltpu.VMEM((MOBA_HEADS, t, LANES), F32)] * 4)
    return pl.pallas_call(
        _moba_attn_kernel, grid_spec=grid_spec, out_shape=jax.ShapeDtypeStruct((B, S, W), BF16),
        compiler_params=_params(("arbitrary", "arbitrary")), name="moba_attn_prompt",
    )(qi_tbl, ki_tbl, q, k, v, kmean_pad, mg)


def _cd_out_kernel(cg_ref, a_ref, hp_ref, wo_ref, gf_ref, y_out):
    half = wo_ref.shape[0] // 2
    h = hp_ref[0] + _dot(cg_ref[0], wo_ref[:half, :]) + _dot(a_ref[0], wo_ref[half:, :])
    y_out[0] = _rms(h, gf_ref[...])


def _cd_out(cg, a, hp, w_out, gf, tm):
    B, S, D = hp.shape
    tok = lambda width: pl.BlockSpec((1, tm, width), lambda b, i: (b, i, 0))
    return pl.pallas_call(
        _cd_out_kernel, grid=(B, S // tm),
        in_specs=[tok(cg.shape[2]), tok(a.shape[2]), tok(D), _const_spec(w_out.shape),
                  _const_spec((1, D))],
        out_specs=tok(D), out_shape=jax.ShapeDtypeStruct((B, S, D), F32),
        compiler_params=_params(("arbitrary", "arbitrary")), name="cd_out",
    )(cg, a, hp, w_out, gf)


QD_WIDTH = 2 * LANES


def _prep_absorb(w_ukv):
    w_uk = w_ukv[..., :MLA_NOPE]
    to_lat = jnp.transpose(w_uk, (1, 2, 0))
    top = jnp.concatenate([to_lat, jnp.zeros((MLA_HEADS, MLA_NOPE, HEAD_PAD), w_ukv.dtype)], axis=2)
    eye = jnp.eye(HEAD_PAD, dtype=w_ukv.dtype)[MLA_NOPE:]
    bot = jnp.concatenate([jnp.zeros((HEAD_PAD - MLA_NOPE, MLA_KV_RANK), w_ukv.dtype), eye], axis=1)
    bot = jnp.broadcast_to(bot[None], (MLA_HEADS,) + bot.shape)
    return jnp.concatenate([top, bot], axis=1).astype(BF16)


def _ab_in_sample_kernel(x_ref, c_ref, g_ref, wa_ref, wb_ref, qg_ref, kvg_ref, wuq_ref, absorb_ref,
                         lng_ref, lnb_ref, w00_ref, b0_ref,
                         qd_out, ckv_out, kr128_out, kr_out, vln_out, ga_out, gmb_out):
    c = c_ref[...]
    q_raw, ckv, kr128, zb = _ab_front(x_ref[...], c, g_ref[...], wa_ref[...], wb_ref[...],
                                      qg_ref[...], kvg_ref[...], wuq_ref[...])
    ckv_out[...] = ckv
    kr128_out[...] = kr128
    kr_out[...] = kr128[:, MLA_NOPE:MLA_NOPE + MLA_ROPE]
    for h in range(MLA_HEADS):
        qh = _rope_mix(q_raw[:, h * HEAD_PAD:(h + 1) * HEAD_PAD], c[0], c[1]).astype(BF16)
        qd_out[:, h * QD_WIDTH:(h + 1) * QD_WIDTH] = _dot(qh, absorb_ref[h]).astype(BF16)
    w = GM_WIDTH
    ga_out[...] = _silu(zb[:, :w]).astype(BF16)
    v_ln = _layernorm(zb[:, 2 * w:3 * w], lng_ref[...], lnb_ref[...])
    vln_out[...] = v_ln
    s = v_ln * w00_ref[...] + b0_ref[...]
    gmb_out[...] = (zb[:, w:2 * w] * s * _silu(zb[:, 3 * w:])).astype(BF16)


def _ab_in_sample(x, tables, norm_g, w_a, w_b, qg, kvg, w_uq_pad, absorb, ln_g, ln_b, w00, b0):
    n = x.shape[0]
    out_shape = [
        jax.ShapeDtypeStruct((n, MLA_HEADS * QD_WIDTH), BF16),
        jax.ShapeDtypeStruct((n, MLA_KV_RANK), F32), jax.ShapeDtypeStruct((n, HEAD_PAD), F32),
        jax.ShapeDtypeStruct((n, MLA_ROPE), F32), jax.ShapeDtypeStruct((n, GM_WIDTH), F32),
        jax.ShapeDtypeStruct((n, MLA_WIDTH), BF16), jax.ShapeDtypeStruct((n, GM_WIDTH), BF16)]
    return pl.pallas_call(
        _ab_in_sample_kernel, out_shape=out_shape,
        compiler_params=pltpu.CompilerParams(vmem_limit_bytes=VMEM_LIMIT), name="ab_in_sample",
    )(x, tables, norm_g, w_a, w_b, qg, kvg, w_uq_pad, absorb, ln_g, ln_b, w00, b0)


MLA_DECODE_CHUNK_PAGES = 16


def _mla_decode_kernel(pt_ref, qd_ref, cnew_ref, rnew_ref, ckv_hbm, kr_hbm, o_ref,
                       ckv_buf, kr_buf, s_ref, sem, *, layer):
    b = pl.program_id(0)
    nb = pl.num_programs(0)
    n_pages, page = ckv_buf.shape[1], ckv_buf.shape[2]
    slot = b % 2

    def page_copies(seq, slot_, p):
        phys = pt_ref[seq, p]
        return (pltpu.make_async_copy(ckv_hbm.at[layer, phys], ckv_buf.at[slot_, p], sem.at[0, slot_]),
                pltpu.make_async_copy(kr_hbm.at[layer, phys], kr_buf.at[slot_, p], sem.at[1, slot_]))

    def fetch(seq, slot_):
        def body(p, carry):
            for cp in page_copies(seq, slot_, p):
                cp.start()
            return carry
        lax.fori_loop(0, n_pages, body, 0)

    @pl.when(b == 0)
    def _():
        fetch(0, 0)

    @pl.when(b + 1 < nb)
    def _():
        fetch(b + 1, 1 - slot)

    def wait_body(p, carry):
        for cp in page_copies(b, slot, p):
            cp.wait()
        return carry
    lax.fori_loop(0, n_pages, wait_body, 0)

    qd = qd_ref[0]
    q_lat = qd[:, :MLA_KV_RANK]
    q_pad = qd[:, MLA_KV_RANK:]
    q_rope = q_pad[:, MLA_NOPE:MLA_NOPE + MLA_ROPE]
    cp_ = MLA_DECODE_CHUNK_PAGES
    rows = cp_ * page
    n_chunks = n_pages // cp_

    def load_chunk(ci):
        pages = pl.ds(pl.multiple_of(ci * cp_, cp_), cp_)
        c = ckv_buf[slot, pages].reshape(rows, MLA_KV_RANK).astype(BF16)
        r = kr_buf[slot, pages].reshape(rows, MLA_ROPE).astype(BF16)
        return c, r

    def score_body(ci, m):
        c, r = load_chunk(ci)
        s = _dot_nt(q_lat, c) + _dot_nt(q_rope, r)
        s_ref[:, pl.ds(pl.multiple_of(ci * rows, rows), rows)] = s
        return jnp.maximum(m, jnp.max(s, axis=1, keepdims=True))

    c_new = cnew_ref[0]
    s_new = (jnp.sum(q_lat.astype(F32) * c_new, axis=1, keepdims=True)
             + jnp.sum(q_pad.astype(F32) * rnew_ref[0], axis=1, keepdims=True))
    m = lax.fori_loop(0, n_chunks, score_body, s_new)

    def value_body(ci, carry):
        l, o = carry
        c, _ = load_chunk(ci)
        p = jnp.exp(s_ref[:, pl.ds(pl.multiple_of(ci * rows, rows), rows)] - m)
        return l + jnp.sum(p, axis=1, keepdims=True), o + _dot(p.astype(BF16), c)

    p_new = jnp.exp(s_new - m)
    l, o = lax.fori_loop(0, n_chunks, value_body, (p_new, p_new * c_new))
    o_ref[0] = o / l


def _mla_decode(page_table, qd, ckv_new, kr128_new, cache_ckv, cache_kr, layer):
    n, n_pages = page_table.shape
    page = cache_ckv.shape[2]
    assert n_pages % MLA_DECODE_CHUNK_PAGES == 0
    grid_spec = pltpu.PrefetchScalarGridSpec(
        num_scalar_prefetch=1, grid=(n,),
        in_specs=[pl.BlockSpec((1, MLA_HEADS, QD_WIDTH), lambda b, pt: (b, 0, 0)),
                  pl.BlockSpec((1, 1, MLA_KV_RANK), lambda b, pt: (b, 0, 0)),
                  pl.BlockSpec((1, 1, HEAD_PAD), lambda b, pt: (b, 0, 0)),
                  pl.BlockSpec(memory_space=pl.ANY), pl.BlockSpec(memory_space=pl.ANY)],
        out_specs=pl.BlockSpec((1, MLA_HEADS, MLA_KV_RANK), lambda b, pt: (b, 0, 0)),
        scratch_shapes=[pltpu.VMEM((2, n_pages, page, MLA_KV_RANK), F32),
                        pltpu.VMEM((2, n_pages, page, MLA_ROPE), F32),
                        pltpu.VMEM((MLA_HEADS, n_pages * page), F32),
                        pltpu.SemaphoreType.DMA((2, 2))])
    return pl.pallas_call(
        functools.partial(_mla_decode_kernel, layer=layer), grid_spec=grid_spec,
        out_shape=jax.ShapeDtypeStruct((n, MLA_HEADS, MLA_KV_RANK), F32),
        compiler_params=_params(("arbitrary",)), name="mla_decode",
    )(page_table, qd.reshape(n, MLA_HEADS, QD_WIDTH), ckv_new.reshape(n, 1, MLA_KV_RANK),
      kr128_new.reshape(n, 1, HEAD_PAD), cache_ckv, cache_kr)


def _mid_sample_kernel(ol_ref, wuv_ref, ga_ref, gm_ref, x_ref, wo_ref, g2_ref, w2_ref, cw_ref,
                       s0_ref, s1_ref,
                       hp_out, cg_out, q_out, k_out, v_out, mg_out, pre_out):
    w = SC_WIDTH
    half = wo_ref.shape[0] // 2
    att = _dot(ol_ref[...].astype(BF16), wuv_ref[...])
    a = (att * ga_ref[...].astype(F32)).astype(BF16)
    hp = x_ref[...] + _dot(a, wo_ref[:half, :]) + _dot(gm_ref[...], wo_ref[half:, :])
    hp_out[...] = hp
    z = _cd_front(hp, g2_ref[...], w2_ref[...])
    pre = z[:, 2 * w:3 * w] * z[:, :w]
    pre_out[...] = pre
    cw = cw_ref[...]
    conv = cw[0:1] * s0_ref[...] + cw[1:2] * s1_ref[...] + cw[2:3] * pre
    cg_out[...] = (z[:, w:2 * w] * conv * _silu(z[:, 3 * w:4 * w])).astype(BF16)
    q_out[...] = z[:, 4 * w:5 * w] * MOBA_SCALE
    k_out[...] = z[:, 5 * w:6 * w]
    v_out[...] = z[:, 6 * w:7 * w]
    mg_out[...] = _silu(z[:, 7 * w:])


def _mid_sample(o_lat, w_uv_bd, ga, gm, x, w_out, g2, w2, conv_w, s0, s1):
    n, D = x.shape
    w = SC_WIDTH
    ff = lambda: jax.ShapeDtypeStruct((n, w), F32)
    bf = lambda: jax.ShapeDtypeStruct((n, w), BF16)
    out_shape = [jax.ShapeDtypeStruct((n, D), F32), bf(), ff(), ff(), ff(), ff(), ff()]
    return pl.pallas_call(
        _mid_sample_kernel, out_shape=out_shape,
        compiler_params=pltpu.CompilerParams(vmem_limit_bytes=VMEM_LIMIT), name="mid_sample",
    )(o_lat, w_uv_bd, ga, gm, x, w_out, g2, w2, conv_w, s0, s1)


KSUM_CHUNK_PAGES = 8
KSUM_RING = 3


def _moba_ksum_kernel(pt_ref, k_hbm, o_ref, buf, sem, *, layer, pages_per_block):
    b = pl.program_id(0)
    nb = pl.num_programs(0)
    n_pages = pt_ref.shape[1]
    cp_ = KSUM_CHUNK_PAGES
    n_chunks = n_pages // cp_
    total = nb * n_chunks

    def copies(g, slot):
        seq = g // n_chunks
        c = g - seq * n_chunks
        return [pltpu.make_async_copy(k_hbm.at[layer, pt_ref[seq, c * cp_ + i]], buf.at[slot, i],
                                      sem.at[slot]) for i in range(cp_)]

    def start(g):
        for cp in copies(g, g % KSUM_RING):
            cp.start()

    @pl.when(b == 0)
    def _():
        for g in range(KSUM_RING - 1):
            start(g)

    def body(c, carry):
        g = b * n_chunks + c
        slot = g % KSUM_RING

        @pl.when(g + KSUM_RING - 1 < total)
        def _():
            start(g + KSUM_RING - 1)

        for cp in copies(g, slot):
            cp.wait()
        bpc = cp_ // pages_per_block
        for j in range(bpc):
            acc = jnp.sum(buf[slot, j * pages_per_block], axis=0)
            for r in range(1, pages_per_block):
                acc = acc + jnp.sum(buf[slot, j * pages_per_block + r], axis=0)
            o_ref[0, c * bpc + j] = acc
        return carry

    lax.fori_loop(0, n_chunks, body, 0)


def _moba_ksum(page_table, cache_k, layer):
    n, n_pages = page_table.shape
    _, _, page, H, Dh = cache_k.shape
    ppb = MOBA_BLOCK // page
    assert n_pages % KSUM_CHUNK_PAGES == 0 and KSUM_CHUNK_PAGES % ppb == 0
    assert n * (n_pages // KSUM_CHUNK_PAGES) >= KSUM_RING
    nblk = n_pages // ppb
    grid_spec = pltpu.PrefetchScalarGridSpec(
        num_scalar_prefetch=1, grid=(n,),
        in_specs=[pl.BlockSpec(memory_space=pl.ANY)],
        out_specs=pl.BlockSpec((1, nblk, H, Dh), lambda b, pt: (b, 0, 0, 0)),
        scratch_shapes=[pltpu.VMEM((KSUM_RING, KSUM_CHUNK_PAGES, page, H, Dh), F32),
                        pltpu.SemaphoreType.DMA((KSUM_RING,))])
    return pl.pallas_call(
        functools.partial(_moba_ksum_kernel, layer=layer, pages_per_block=ppb), grid_spec=grid_spec,
        out_shape=jax.ShapeDtypeStruct((n, nblk, H, Dh), F32),
        compiler_params=_params(("arbitrary",)), name="moba_ksum",
    )(page_table, cache_k)


def _moba_select_kernel(ks_ref, q_ref, idx_out):
    ks = ks_ref[0]
    gate = jnp.sum(ks * q_ref[...], axis=-1)
    blk = lax.broadcasted_iota(jnp.int32, gate.shape, 0)
    n = gate.shape[0]
    for t in range(MOBA_TOPK):
        mx = jnp.max(gate, axis=0, keepdims=True)
        first = jnp.min(jnp.where(gate == mx, blk, n), axis=0, keepdims=True)
        idx_out[0, t:t + 1, :] = first
        gate = jnp.where(blk == first, -jnp.inf, gate)


def _moba_select(ksum, q):
    n, nblk, H, Dh = ksum.shape
    return pl.pallas_call(
        _moba_select_kernel, grid=(n,),
        in_specs=[pl.BlockSpec((1, nblk, H, Dh), lambda b: (b, 0, 0, 0)),
                  pl.BlockSpec((1, H, Dh), lambda b: (b, 0, 0))],
        out_specs=pl.BlockSpec((1, MOBA_TOPK, H), lambda b: (b, 0, 0)),
        out_shape=jax.ShapeDtypeStruct((n, MOBA_TOPK, H), jnp.int32),
        compiler_params=_params(("arbitrary",)), name="moba_select",
    )(ksum, q)


def _moba_decode_kernel(pt_ref, idx_ref, q_ref, kn_ref, vn_ref, mg_ref, k_hbm, v_hbm, o_ref,
                        kbuf, vbuf, sem, *, layer, pages_per_block):
    b = pl.program_id(0)
    nb = pl.num_programs(0)
    H = q_ref.shape[1]
    slot = b % 2
    n_sel_pages = MOBA_TOPK * pages_per_block

    def copies(seq, slot_):
        out = []
        for h in range(H):
            for t in range(MOBA_TOPK):
                blk = idx_ref[seq, h * MOBA_TOPK + t]
                for r in range(pages_per_block):
                    phys = pt_ref[seq, blk * pages_per_block + r]
                    j = t * pages_per_block + r
                    out.append(pltpu.make_async_copy(k_hbm.at[layer, phys, :, h, :],
                                                     kbuf.at[slot_, h, j], sem.at[0, slot_]))
                    out.append(pltpu.make_async_copy(v_hbm.at[layer, phys, :, h, :],
                                                     vbuf.at[slot_, h, j], sem.at[1, slot_]))
        return out

    @pl.when(b == 0)
    def _():
        for cp in copies(0, 0):
            cp.start()

    @pl.when(b + 1 < nb)
    def _():
        for cp in copies(b + 1, 1 - slot):
            cp.start()

    for cp in copies(b, slot):
        cp.wait()

    q = q_ref[0]
    qb = q.astype(BF16)
    k_new, v_new = kn_ref[0], vn_ref[0]
    s_new = jnp.sum(q * k_new, axis=1, keepdims=True)
    row = lax.broadcasted_iota(jnp.int32, q.shape, 0)
    page = kbuf.shape[3]
    out = jnp.zeros(q.shape, F32)
    for h in range(H):
        kh = kbuf[slot, h].reshape(n_sel_pages * page, q.shape[1]).astype(BF16)
        vh = vbuf[slot, h].reshape(n_sel_pages * page, q.shape[1]).astype(BF16)
        s = _dot_nt(qb, kh)
        m = jnp.maximum(jnp.max(s, axis=1, keepdims=True), s_new)
        p = jnp.exp(s - m)
        p_new = jnp.exp(s_new - m)
        l = jnp.sum(p, axis=1, keepdims=True) + p_new
        o = (_dot(p.astype(BF16), vh) + p_new * v_new) / l
        out = jnp.where(row == h, o, out)
    o_ref[0] = out * mg_ref[0]


def _moba_decode(page_table, idx, q, k_new, v_new, mg, cache_k, cache_v, layer):
    n, n_pages = page_table.shape
    _, _, page, H, Dh = cache_k.shape
    ppb = MOBA_BLOCK // page
    tok = pl.BlockSpec((1, H, Dh), lambda b, pt, ix: (b, 0, 0))
    grid_spec = pltpu.PrefetchScalarGridSpec(
        num_scalar_prefetch=2, grid=(n,),
        in_specs=[tok, tok, tok, tok, pl.BlockSpec(memory_space=pl.ANY),
                  pl.BlockSpec(memory_space=pl.ANY)],
        out_specs=tok,
        scratch_shapes=[pltpu.VMEM((2, H, MOBA_TOPK * ppb, page, Dh), F32),
                        pltpu.VMEM((2, H, MOBA_TOPK * ppb, page, Dh), F32),
                        pltpu.SemaphoreType.DMA((2, 2))])
    return pl.pallas_call(
        functools.partial(_moba_decode_kernel, layer=layer, pages_per_block=ppb),
        grid_spec=grid_spec, out_shape=jax.ShapeDtypeStruct((n, H, Dh), F32),
        compiler_params=_params(("arbitrary",)), name="moba_decode",
    )(page_table, idx, q, k_new, v_new, mg, cache_k, cache_v)


PROJ_TILE = 512
ATTN_TILE = 512


def kernel(x_prompt, x_sample, cache_mla_ckv, cache_mla_krope, cache_moba_k, cache_moba_v, state_conv,
           page_table, ab_norm_g, ab_w_in, ab_q_norm_g, ab_kv_norm_g, ab_w_uq, ab_w_ukv, ab_gm_ln_g,
           ab_gm_ln_b, ab_gm_ws, ab_gm_bs, ab_w_out, cd_norm_g, cd_w_in, cd_conv_w, cd_w_out,
           final_norm_g):
    B, S, D = x_prompt.shape
    DB, T, _ = x_sample.shape
    n_pages = page_table.shape[1]
    page = cache_mla_ckv.shape[2]
    past_len = n_pages * page
    assert T == 1 and ab_w_in.shape[0] == 1 and cd_w_in.shape[0] == 1
    assert past_len % MOBA_BLOCK == 0 and past_len // MOBA_BLOCK >= MOBA_TOPK
    assert S % PROJ_TILE == 0 and S % ATTN_TILE == 0 and PROJ_TILE % MOBA_BLOCK == 0
    li = 0
    row = lambda v: v.reshape(1, -1)

    w_a, w_b, w_uq_pad, w_uk_pad, w_uv_flat = _prep_ab_weights(ab_w_in[li], ab_w_uq[li], ab_w_ukv[li])
    absorb = _prep_absorb(ab_w_ukv[li])
    w_uv = ab_w_ukv[li][..., MLA_NOPE:]
    w_uv_bd = (jnp.eye(MLA_HEADS, dtype=F32)[:, None, :, None] * jnp.transpose(w_uv, (1, 0, 2))[:, :, None, :])
    w_uv_bd = w_uv_bd.reshape(MLA_HEADS * MLA_KV_RANK, MLA_WIDTH).astype(BF16)
    ws = ab_gm_ws[li]
    bs_exp = jnp.repeat(ab_gm_bs[li].T, GM_GROUP_DIM, axis=1)
    w00 = jnp.repeat(ws[:, 0, 0], GM_GROUP_DIM).reshape(1, GM_WIDTH)
    b0 = bs_exp[0:1]
    w_out0 = ab_w_out[li].astype(BF16)
    w2 = cd_w_in[li].astype(BF16)
    w_out1 = cd_w_out[li].astype(BF16)
    conv_w = cd_conv_w[li]
    tab_p = _rope_tables(jnp.arange(S, dtype=jnp.int32))
    tab_s = _rope_tables(jnp.full((DB,), past_len, jnp.int32))
    g0, qg, kvg = row(ab_norm_g[li]), row(ab_q_norm_g[li]), row(ab_kv_norm_g[li])
    ln_g, ln_b = row(ab_gm_ln_g[li]), row(ab_gm_ln_b[li])
    g2, gf = row(cd_norm_g[li]), row(final_norm_g)

    q, k, v, ckv_p, kr_p, ga, gmb = _ab_in_prompt(
        x_prompt, tab_p, g0, w_a, w_b, qg, kvg, w_uq_pad, w_uk_pad, w_uv_flat, ln_g, ln_b, ws, bs_exp,
        PROJ_TILE)
    a0 = _mla_attn_prompt(q, k, v, ga, ATTN_TILE)
    hp, cg, q1, k1b, v1b, mg, k1, v1, kmean, conv_p = _mid_prompt(
        a0, gmb, x_prompt, w_out0, g2, w2, conv_w, PROJ_TILE)
    nb = S // MOBA_BLOCK
    assert nb <= LANES
    kmean_pad = jnp.pad(kmean.reshape(B, nb, MOBA_WIDTH), ((0, 0), (0, LANES - nb), (0, 0)))
    a1 = _moba_attn_prompt(q1, k1b, v1b, kmean_pad, mg)
    y_prompt = _cd_out(cg, a1, hp, w_out1, gf, PROJ_TILE)

    xs = x_sample.reshape(DB, D)
    qd, ckv_s, kr128_s, kr_s, vln_s, ga_s, gmb_s = _ab_in_sample(
        xs, tab_s, g0, w_a, w_b, qg, kvg, w_uq_pad, absorb, ln_g, ln_b, w00, b0)
    o_lat = _mla_decode(page_table, qd, ckv_s, kr128_s, cache_mla_ckv, cache_mla_krope, li)
    hs, cg_s, q_s, k_s, v_s, mg_s, pre_s = _mid_sample(
        o_lat.reshape(DB, MLA_HEADS * MLA_KV_RANK), w_uv_bd, ga_s, gmb_s, xs, w_out0, g2, w2, conv_w,
        state_conv[li, :, 0], state_conv[li, :, 1])
    ksum = _moba_ksum(page_table, cache_moba_k, li)
    heads = lambda t: t.reshape(DB, MOBA_HEADS, MOBA_HEAD_DIM)
    idx = _moba_select(ksum, heads(q_s))
    idx = jnp.transpose(idx, (0, 2, 1)).reshape(DB, MOBA_HEADS * MOBA_TOPK)
    a1_s = _moba_decode(page_table, idx, heads(q_s), heads(k_s), heads(v_s),
                        heads(mg_s), cache_moba_k, cache_moba_v, li)
    y_sample = _cd_out(cg_s[None], a1_s.reshape(1, DB, MOBA_WIDTH).astype(BF16), hs[None], w_out1, gf, DB)

    return (y_prompt, y_sample.reshape(DB, 1, D),
            ckv_p[None], kr_p[None],
            ckv_s.reshape(1, DB, 1, MLA_KV_RANK), kr_s.reshape(1, DB, 1, MLA_ROPE),
            vln_s.reshape(1, DB, 1, GM_WIDTH),
            conv_p[None], jnp.stack([state_conv[li, :, 1], pre_s], axis=1)[None],
            k1.reshape(1, B, S, MOBA_HEADS, MOBA_HEAD_DIM), v1.reshape(1, B, S, MOBA_HEADS, MOBA_HEAD_DIM),
            k_s.reshape(1, DB, 1, MOBA_HEADS, MOBA_HEAD_DIM), v_s.reshape(1, DB, 1, MOBA_HEADS, MOBA_HEAD_DIM))
```

```python
import functools

import jax
import jax.numpy as jnp
import numpy as np
from jax import lax
from jax.experimental import pallas as pl
from jax.experimental.pallas import tpu as pltpu

F32 = jnp.float32
BF16 = jnp.bfloat16

MLA_HEADS = 8
MLA_NOPE = 64
MLA_ROPE = 32
MLA_V = 64
MLA_Q_RANK = 256
MLA_KV_RANK = 128
MLA_WIDTH = MLA_HEADS * MLA_V
MLA_SCALE = (MLA_NOPE + MLA_ROPE) ** -0.5
ROPE_THETA = 10000.0
GM_GROUPS = 8
GM_WIDTH = 512
GM_GROUP_DIM = GM_WIDTH // GM_GROUPS
GM_CHUNK = 128
SC_WIDTH = 512
CONV_LEN = 3
MOBA_HEADS = 8
MOBA_HEAD_DIM = 64
MOBA_WIDTH = MOBA_HEADS * MOBA_HEAD_DIM
MOBA_BLOCK = 256
MOBA_TOPK = 3
MOBA_SCALE = MOBA_HEAD_DIM ** -0.5
NORM_EPS = 1e-6

LANES = 128
SUBLANES = 8
HEAD_PAD = 128
ROLL_SPARE_TO_ROPE = LANES - MLA_ROPE
M_INIT = -1e30
MASKED = -2e30
LOG2E = 1.4426950408889634
VMEM_LIMIT = 56 * 1024 * 1024


def _dot(a, b):
    return jnp.dot(a, b, preferred_element_type=F32)


def _dot_nt(a, b, precision=None):
    return lax.dot_general(a, b, (((1,), (1,)), ((), ())), precision=precision,
                           preferred_element_type=F32)


def _rms(x, g):
    return x * lax.rsqrt(jnp.mean(x * x, axis=-1, keepdims=True) + NORM_EPS) * g


def _layernorm(x, g, b):
    mu = jnp.mean(x, axis=-1, keepdims=True)
    d = x - mu
    var = jnp.mean(d * d, axis=-1, keepdims=True)
    return d * lax.rsqrt(var + NORM_EPS) * g + b


def _silu(x):
    return x * jax.nn.sigmoid(x)


def _rope_mix(raw, c_keep, c_rot):
    return raw * c_keep + pltpu.roll(raw, ROLL_SPARE_TO_ROPE, 1) * c_rot


def _const_spec(shape):
    nd = len(shape)
    return pl.BlockSpec(shape, lambda *_: (0,) * nd)


def _params(sem):
    return pltpu.CompilerParams(dimension_semantics=sem, vmem_limit_bytes=VMEM_LIMIT)


def _rot_half_cols(w):
    half = w.shape[-1] // 2
    return jnp.concatenate([-w[..., half:], w[..., :half]], axis=-1)


def _prep_ab_weights(w_in, w_uq, w_ukv):
    d = w_in.shape[0]
    o = np.cumsum((0, MLA_Q_RANK, MLA_KV_RANK, MLA_ROPE))
    w_cq, w_ckv, w_kr = (w_in[:, o[i]:o[i + 1]] for i in range(3))
    w_kr_blk = jnp.concatenate(
        [jnp.zeros((d, MLA_NOPE), w_in.dtype), w_kr, _rot_half_cols(w_kr)], axis=1)
    w_a = jnp.concatenate([w_cq, w_ckv, w_kr_blk], axis=1).astype(BF16)
    w_b = w_in[:, o[3]:].astype(BF16)
    uq_nope, uq_rope = w_uq[..., :MLA_NOPE], w_uq[..., MLA_NOPE:]
    w_uq_pad = jnp.concatenate([uq_nope, uq_rope, _rot_half_cols(uq_rope)], axis=-1)
    w_uq_pad = w_uq_pad.reshape(MLA_Q_RANK, MLA_HEADS * HEAD_PAD).astype(BF16)
    w_uk, w_uv = w_ukv[..., :MLA_NOPE], w_ukv[..., MLA_NOPE:]
    w_uk_pad = jnp.concatenate([w_uk, jnp.zeros_like(w_uk)], axis=-1)
    w_uk_pad = w_uk_pad.reshape(MLA_KV_RANK, MLA_HEADS * HEAD_PAD).astype(BF16)
    w_uv_flat = w_uv.reshape(MLA_KV_RANK, MLA_WIDTH).astype(BF16)
    return w_a, w_b, w_uq_pad, w_uk_pad, w_uv_flat


def _rope_tables(pos):
    half = MLA_ROPE // 2
    inv = ROPE_THETA ** (-jnp.arange(half, dtype=F32) / half)
    ang = pos.astype(F32)[:, None] * inv[None, :]
    cos, sin = jnp.cos(ang), jnp.sin(ang)
    n = pos.shape[0]
    cos2 = jnp.concatenate([cos, cos], axis=1)
    sin2 = jnp.concatenate([sin, sin], axis=1)
    zeros_nope = jnp.zeros((n, MLA_NOPE), F32)
    ones_nope = jnp.ones((n, MLA_NOPE), F32)
    spare = jnp.zeros((n, HEAD_PAD - MLA_NOPE - MLA_ROPE), F32)
    q_keep = jnp.concatenate([ones_nope, cos2, spare], axis=1) * (MLA_SCALE * LOG2E)
    q_rot = jnp.concatenate([zeros_nope, sin2, spare], axis=1) * (MLA_SCALE * LOG2E)
    k_keep = jnp.concatenate([zeros_nope, cos2, spare], axis=1)
    k_rot = jnp.concatenate([zeros_nope, sin2, spare], axis=1)
    return jnp.stack([q_keep, q_rot, k_keep, k_rot])


def _ab_front(x, c, g, wa, wb, qg, kvg, wuq):
    h = _rms(x, g).astype(BF16)
    za = _dot(h, wa)
    zb = _dot(h, wb)
    cqn = _rms(za[:, :MLA_Q_RANK], qg).astype(BF16)
    q_raw = _dot(cqn, wuq)
    ckv = _rms(za[:, MLA_Q_RANK:MLA_Q_RANK + MLA_KV_RANK], kvg)
    kr128 = _rope_mix(za[:, MLA_Q_RANK + MLA_KV_RANK:], c[2], c[3])
    return q_raw, ckv, kr128, zb


def _ab_in_kernel(x_ref, c_ref, g_ref, wa_ref, wb_ref, qg_ref, kvg_ref, wuq_ref, wuk_ref, wuv_ref,
                  lng_ref, lnb_ref, ws_ref, bse_ref,
                  q_out, k_out, v_out, ckv_out, kr_out, ga_out, gmb_out):
    tm = x_ref.shape[1]
    c = c_ref[...]
    q_raw, ckv, kr128, zb = _ab_front(x_ref[0], c, g_ref[...], wa_ref[...], wb_ref[...],
                                      qg_ref[...], kvg_ref[...], wuq_ref[...])
    ckv_out[0] = ckv
    kr_out[0] = kr128[:, MLA_NOPE:MLA_NOPE + MLA_ROPE]
    ckv_b = ckv.astype(BF16)
    kn = _dot(ckv_b, wuk_ref[...])
    v_out[0] = _dot(ckv_b, wuv_ref[...]).astype(BF16)
    for h in range(MLA_HEADS):
        sl = slice(h * HEAD_PAD, (h + 1) * HEAD_PAD)
        q_out[0, :, sl] = _rope_mix(q_raw[:, sl], c[0], c[1]).astype(BF16)
        k_out[0, :, sl] = (kn[:, sl] + kr128).astype(BF16)
    w = GM_WIDTH
    ga_out[0] = _silu(zb[:, :w]).astype(BF16)
    u, v, g_gm = zb[:, w:2 * w], zb[:, 2 * w:3 * w], zb[:, 3 * w:]
    v_ln = _layernorm(v, lng_ref[...], lnb_ref[...]).astype(BF16)
    row = lax.broadcasted_iota(jnp.int32, (GM_CHUNK, GM_CHUNK), 0)
    col = lax.broadcasted_iota(jnp.int32, (GM_CHUNK, GM_CHUNK), 1)
    lane = lax.broadcasted_iota(jnp.int32, (GM_CHUNK, LANES), 1)
    ws_tril = [jnp.where(col <= row, ws_ref[gi], 0.0).astype(BF16) for gi in range(GM_GROUPS)]
    groups_per_vreg = LANES // GM_GROUP_DIM
    assert groups_per_vreg == 2
    for ci in range(tm // GM_CHUNK):
        rows = slice(ci * GM_CHUNK, (ci + 1) * GM_CHUNK)
        for j in range(w // LANES):
            cols = slice(j * LANES, (j + 1) * LANES)
            vb = v_ln[rows, cols]
            s_lo = _dot(ws_tril[2 * j], vb)
            s_hi = _dot(ws_tril[2 * j + 1], vb)
            s = jnp.where(lane < GM_GROUP_DIM, s_lo, s_hi) + bse_ref[:, cols]
            gm = u[rows, cols] * s * _silu(g_gm[rows, cols])
            gmb_out[0, rows, cols] = gm.astype(BF16)


def _ab_in_prompt(x, tables, norm_g, w_a, w_b, qg, kvg, w_uq_pad, w_uk_pad, w_uv_flat,
                  ln_g, ln_b, ws, bs_exp, tm):
    B, S, D = x.shape
    grid = (S // tm, B)
    tok = lambda w: pl.BlockSpec((1, tm, w), lambda i, b: (b, i, 0))
    in_specs = [
        tok(D),
        pl.BlockSpec((4, tm, LANES), lambda i, b: (0, i, 0)),
        _const_spec((1, D)), _const_spec(w_a.shape), _const_spec(w_b.shape),
        _const_spec((1, MLA_Q_RANK)), _const_spec((1, MLA_KV_RANK)),
        _const_spec(w_uq_pad.shape), _const_spec(w_uk_pad.shape), _const_spec(w_uv_flat.shape),
        _const_spec((1, GM_WIDTH)), _const_spec((1, GM_WIDTH)),
        _const_spec(ws.shape), _const_spec(bs_exp.shape),
    ]
    hp = MLA_HEADS * HEAD_PAD
    out_shape = [
        jax.ShapeDtypeStruct((B, S, hp), BF16), jax.ShapeDtypeStruct((B, S, hp), BF16),
        jax.ShapeDtypeStruct((B, S, MLA_WIDTH), BF16),
        jax.ShapeDtypeStruct((B, S, MLA_KV_RANK), F32), jax.ShapeDtypeStruct((B, S, MLA_ROPE), F32),
        jax.ShapeDtypeStruct((B, S, MLA_WIDTH), BF16), jax.ShapeDtypeStruct((B, S, GM_WIDTH), BF16),
    ]
    out_specs = [tok(hp), tok(hp), tok(MLA_WIDTH), tok(MLA_KV_RANK), tok(MLA_ROPE),
                 tok(MLA_WIDTH), tok(GM_WIDTH)]
    return pl.pallas_call(
        _ab_in_kernel, grid=grid, in_specs=in_specs, out_specs=out_specs, out_shape=out_shape,
        compiler_params=_params(("arbitrary", "arbitrary")), name="ab_in_prompt",
    )(x, tables, norm_g, w_a, w_b, qg, kvg, w_uq_pad, w_uk_pad, w_uv_flat, ln_g, ln_b, ws, bs_exp)


def _tri_tables(n):
    qi, ki = np.tril_indices(n)
    return jnp.asarray(qi, jnp.int32), jnp.asarray(ki, jnp.int32)


def _with_ones(v_pair):
    return jnp.concatenate([v_pair, jnp.ones(v_pair.shape, v_pair.dtype)], axis=1)


def _softmax_step(s, v_ones, h, m_ref, l_ref, acc_ref):
    tk = s.shape[1]
    m_prev = m_ref[h]
    m_next = jnp.maximum(m_prev, jnp.max(s, axis=1, keepdims=True))
    p = jnp.exp2(s - jnp.concatenate([m_next] * (tk // LANES), axis=1))
    alpha = jnp.exp2(m_prev - m_next)
    pv = _dot(p.astype(BF16), v_ones)
    l_ref[h] = alpha * l_ref[h] + pv[:, LANES:]
    m_ref[h] = m_next
    acc_ref[h] = acc_ref[h] * alpha + pv[:, :LANES]


def _attn_init(m_ref, l_ref, acc_ref):
    m_ref[...] = jnp.full(m_ref.shape, M_INIT, F32)
    l_ref[...] = jnp.zeros(l_ref.shape, F32)
    acc_ref[...] = jnp.zeros(acc_ref.shape, F32)


def _attn_finish(gate_ref, o_ref, l_ref, acc_ref, n_heads):
    tq = acc_ref.shape[1]
    lane = lax.broadcasted_iota(jnp.int32, (tq, LANES), 1)
    for j in range(n_heads // 2):
        lo = acc_ref[2 * j] / l_ref[2 * j]
        hi = acc_ref[2 * j + 1] / l_ref[2 * j + 1]
        cols = slice(j * LANES, (j + 1) * LANES)
        o = jnp.where(lane < LANES // 2, lo, hi) * gate_ref[0, :, cols].astype(F32)
        o_ref[0, :, cols] = o.astype(o_ref.dtype)


def _mla_attn_kernel(qi_ref, ki_ref, q_ref, k_ref, v_ref, ga_ref, o_ref, m_ref, l_ref, acc_ref):
    t = q_ref.shape[1]
    pair = pl.program_id(1)
    qi, ki = qi_ref[pair], ki_ref[pair]

    @pl.when(ki == 0)
    def _():
        _attn_init(m_ref, l_ref, acc_ref)

    def step(diagonal):
        if diagonal:
            row = lax.broadcasted_iota(jnp.int32, (t, t), 0)
            col = lax.broadcasted_iota(jnp.int32, (t, t), 1)
            keep = col <= row
        for h in range(MLA_HEADS):
            sl = slice(h * HEAD_PAD, (h + 1) * HEAD_PAD)
            s = _dot_nt(q_ref[0, :, sl], k_ref[0, :, sl])
            if diagonal:
                s = jnp.where(keep, s, MASKED)
            vp = v_ref[0, :, (h // 2) * LANES:(h // 2 + 1) * LANES]
            _softmax_step(s, _with_ones(vp), h, m_ref, l_ref, acc_ref)

    @pl.when(ki < qi)
    def _():
        step(False)

    @pl.when(ki == qi)
    def _():
        step(True)
        _attn_finish(ga_ref, o_ref, l_ref, acc_ref, MLA_HEADS)


def _mla_attn_prompt(q, k, v, ga, t):
    B, S, _ = q.shape
    n = S // t
    qi_tbl, ki_tbl = _tri_tables(n)
    qmap = lambda b, p, qi, ki: (b, qi[p], 0)
    kmap = lambda b, p, qi, ki: (b, ki[p], 0)
    hp = MLA_HEADS * HEAD_PAD
    grid_spec = pltpu.PrefetchScalarGridSpec(
        num_scalar_prefetch=2, grid=(B, qi_tbl.shape[0]),
        in_specs=[pl.BlockSpec((1, t, hp), qmap), pl.BlockSpec((1, t, hp), kmap),
                  pl.BlockSpec((1, t, MLA_WIDTH), kmap), pl.BlockSpec((1, t, MLA_WIDTH), qmap)],
        out_specs=pl.BlockSpec((1, t, MLA_WIDTH), qmap),
        scratch_shapes=[pltpu.VMEM((MLA_HEADS, t, LANES), F32), pltpu.VMEM((MLA_HEADS, t, LANES), F32),
                        pltpu.VMEM((MLA_HEADS, t, LANES), F32)])
    return pl.pallas_call(
        _mla_attn_kernel, grid_spec=grid_spec, out_shape=jax.ShapeDtypeStruct((B, S, MLA_WIDTH), BF16),
        compiler_params=_params(("arbitrary", "arbitrary")), name="mla_attn_prompt",
    )(qi_tbl, ki_tbl, q, k, v, ga)


def _cd_front(hp, g2, w2):
    h2 = _rms(hp, g2).astype(BF16)
    return _dot(h2, w2)


def _mid_kernel(a_ref, gm_ref, x_ref, wo_ref, g2_ref, w2_ref, cw_ref,
                hp_out, cg_out, q_out, kb_out, vb_out, mg_out, kf_out, vf_out, km_out, cp_out,
                pre_ref):
    tm = x_ref.shape[1]
    i = pl.program_id(1)
    w = SC_WIDTH
    half = wo_ref.shape[0] // 2
    hp = x_ref[0] + _dot(a_ref[0], wo_ref[:half, :]) + _dot(gm_ref[0], wo_ref[half:, :])
    hp_out[0] = hp
    z = _cd_front(hp, g2_ref[...], w2_ref[...])
    pre = z[:, 2 * w:3 * w] * z[:, :w]

    @pl.when(i == 0)
    def _():
        pre_ref[:SUBLANES, :] = jnp.zeros((SUBLANES, w), F32)

    pre_ref[SUBLANES:, :] = pre
    cw = cw_ref[...]
    conv = (cw[0:1] * pre_ref[SUBLANES - 2:SUBLANES - 2 + tm, :]
            + cw[1:2] * pre_ref[SUBLANES - 1:SUBLANES - 1 + tm, :] + cw[2:3] * pre)
    tail = pre_ref[tm:tm + SUBLANES, :]
    pre_ref[:SUBLANES, :] = tail
    cp_out[0] = tail[SUBLANES - (CONV_LEN - 1):, :]
    cg_out[0] = (z[:, w:2 * w] * conv * _silu(z[:, 3 * w:4 * w])).astype(BF16)
    q_out[0] = (z[:, 4 * w:5 * w] * (MOBA_SCALE * LOG2E)).astype(BF16)
    k = z[:, 5 * w:6 * w]
    v = z[:, 6 * w:7 * w]
    kf_out[0] = k
    vf_out[0] = v
    kb_out[0] = k.astype(BF16)
    vb_out[0] = v.astype(BF16)
    mg_out[0] = _silu(z[:, 7 * w:]).astype(BF16)
    for j in range(tm // MOBA_BLOCK):
        blk = k[j * MOBA_BLOCK:(j + 1) * MOBA_BLOCK]
        km_out[0, j] = jnp.sum(blk, axis=0, keepdims=True) * (1.0 / MOBA_BLOCK)


def _mid_prompt(a, gm, x, w_out, g2, w2, conv_w, tm):
    B, S, D = x.shape
    w = SC_WIDTH
    nb = S // MOBA_BLOCK
    tok = lambda width: pl.BlockSpec((1, tm, width), lambda b, i: (b, i, 0))
    in_specs = [tok(a.shape[2]), tok(gm.shape[2]), tok(D),
                _const_spec(w_out.shape), _const_spec((1, D)), _const_spec(w2.shape),
                _const_spec(conv_w.shape)]
    bf = lambda: jax.ShapeDtypeStruct((B, S, w), BF16)
    ff = lambda: jax.ShapeDtypeStruct((B, S, w), F32)
    out_shape = [jax.ShapeDtypeStruct((B, S, D), F32), bf(), bf(), bf(), bf(), bf(), ff(), ff(),
                 jax.ShapeDtypeStruct((B, nb, 1, w), F32),
                 jax.ShapeDtypeStruct((B, CONV_LEN - 1, w), F32)]
    bpt = tm // MOBA_BLOCK
    out_specs = [tok(D)] + [tok(w)] * 7 + [
        pl.BlockSpec((1, bpt, 1, w), lambda b, i: (b, i, 0, 0)),
        pl.BlockSpec((1, CONV_LEN - 1, w), lambda b, i: (b, 0, 0))]
    return pl.pallas_call(
        _mid_kernel, grid=(B, S // tm), in_specs=in_specs, out_specs=out_specs, out_shape=out_shape,
        scratch_shapes=[pltpu.VMEM((tm + SUBLANES, w), F32)],
        compiler_params=_params(("arbitrary", "arbitrary")), name="mid_prompt",
    )(a, gm, x, w_out, g2, w2, conv_w)


def _select_top_blocks_t(gate_t, blk, own):
    blk_f = blk.astype(F32)
    g = jnp.where(blk < own, gate_t, -jnp.inf)
    allowed = jnp.where(blk == own, 1.0, 0.0)
    for _ in range(MOBA_TOPK):
        mx = jnp.max(g, axis=0, keepdims=True)
        first = jnp.min(jnp.where(g == mx, blk_f, float(LANES)), axis=0, keepdims=True)
        pick = (blk_f == first) & (mx > -jnp.inf)
        allowed = jnp.where(pick, 1.0, allowed)
        g = jnp.where(pick, -jnp.inf, g)
    return allowed


def _moba_attn_kernel(qi_ref, kb_ref, q_ref, k_ref, v_ref, km_ref, mg_ref, o_ref,
                      qa_ref, m_ref, l_ref, acc_ref):
    tq, tk = q_ref.shape[1], k_ref.shape[1]
    q_blocks = tq // tk
    pair = pl.program_id(1)
    qi, kb = qi_ref[pair], kb_ref[pair]
    first_own = qi * q_blocks

    @pl.when(kb == 0)
    def _():
        _attn_init(m_ref, l_ref, acc_ref)
        lane = lax.broadcasted_iota(jnp.int32, (tq, LANES), 1)
        nbp = km_ref.shape[1]
        blk = lax.broadcasted_iota(jnp.int32, (nbp, tq), 0)
        qcol = lax.broadcasted_iota(jnp.int32, (nbp, tq), 1)
        assert tk & (tk - 1) == 0
        own = first_own + lax.shift_right_logical(qcol, tk.bit_length() - 1)
        eye = jnp.where(lax.broadcasted_iota(jnp.int32, (LANES, LANES), 0)
                        == lax.broadcasted_iota(jnp.int32, (LANES, LANES), 1), 1.0, 0.0).astype(BF16)
        for h in range(MOBA_HEADS):
            cols = slice((h // 2) * LANES, (h // 2 + 1) * LANES)
            in_head = (lane >= LANES // 2) if h % 2 else (lane < LANES // 2)
            qm = jnp.where(in_head, q_ref[0, :, cols].astype(F32), 0.0)
            gate_t = _dot_nt(km_ref[0, :, cols], qm, precision=lax.Precision.HIGHEST)
            allowed_t = _select_top_blocks_t(gate_t, blk, own)
            allowed_t = jnp.concatenate(
                [allowed_t, jnp.zeros((LANES - nbp, tq), F32)], axis=0).astype(BF16)
            allowed = jnp.concatenate(
                [_dot_nt(eye, allowed_t[:, c * LANES:(c + 1) * LANES]) for c in range(tq // LANES)],
                axis=0)
            bias = jnp.where(allowed > 0.5, 0.0, MASKED)
            qa_ref[h] = jnp.concatenate([qm.astype(BF16), bias.astype(BF16)], axis=1)

    def step(own_blocks):
        blk_lane = lax.broadcasted_iota(jnp.int32, (tk, LANES), 1)
        onehot = jnp.where(blk_lane == kb, 1.0, 0.0).astype(BF16)
        if own_blocks:
            qpos = qi * tq + lax.broadcasted_iota(jnp.int32, (tq, tk), 0)
            kpos = kb * tk + lax.broadcasted_iota(jnp.int32, (tq, tk), 1)
            keep = kpos <= qpos
        for h in range(MOBA_HEADS):
            cols = slice((h // 2) * LANES, (h // 2 + 1) * LANES)
            k_aug = jnp.concatenate([k_ref[0, :, cols], onehot], axis=1)
            s = _dot_nt(qa_ref[h], k_aug)
            if own_blocks:
                s = jnp.where(keep, s, MASKED)
            _softmax_step(s, _with_ones(v_ref[0, :, cols]), h, m_ref, l_ref, acc_ref)

    @pl.when(kb < first_own)
    def _():
        step(False)

    @pl.when(kb >= first_own)
    def _():
        step(True)

    @pl.when(kb == first_own + q_blocks - 1)
    def _():
        _attn_finish(mg_ref, o_ref, l_ref, acc_ref, MOBA_HEADS)


MOBA_Q_BLOCKS = 2


def _moba_attn_prompt(q, k, v, kmean_pad, mg):
    B, S, W = q.shape
    tk = MOBA_BLOCK
    tq = MOBA_Q_BLOCKS * tk
    assert S % tq == 0
    pairs = [(qi, kb) for qi in range(S // tq) for kb in range(MOBA_Q_BLOCKS * (qi + 1))]
    qi_tbl = jnp.asarray([p[0] for p in pairs], jnp.int32)
    kb_tbl = jnp.asarray([p[1] for p in pairs], jnp.int32)
    qmap = lambda b, p, qi, kb: (b, qi[p], 0)
    kmap = lambda b, p, qi, kb: (b, kb[p], 0)
    grid_spec = pltpu.PrefetchScalarGridSpec(
        num_scalar_prefetch=2, grid=(B, len(pairs)),
        in_specs=[pl.BlockSpec((1, tq, W), qmap), pl.BlockSpec((1, tk, W), kmap),
                  pl.BlockSpec((1, tk, W), kmap),
                  pl.BlockSpec((1, kmean_pad.shape[1], W), lambda b, p, qi, kb: (b, 0, 0)),
                  pl.BlockSpec((1, tq, W), qmap)],
        out_specs=pl.BlockSpec((1, tq, W), qmap),
        scratch_shapes=[pltpu.VMEM((MOBA_HEADS, tq, 2 * LANES), BF16)]
        + [pltpu.VMEM((MOBA_HEADS, tq, LANES), F32)] * 3)
    return pl.pallas_call(
        _moba_attn_kernel, grid_spec=grid_spec, out_shape=jax.ShapeDtypeStruct((B, S, W), BF16),
        compiler_params=_params(("arbitrary", "arbitrary")), name="moba_attn_prompt",
    )(qi_tbl, kb_tbl, q, k, v, kmean_pad, mg)


def _cd_out_kernel(cg_ref, a_ref, hp_ref, wo_ref, gf_ref, y_out):
    half = wo_ref.shape[0] // 2
    h = hp_ref[0] + _dot(cg_ref[0], wo_ref[:half, :]) + _dot(a_ref[0], wo_ref[half:, :])
    y_out[0] = _rms(h, gf_ref[...])


def _cd_out(cg, a, hp, w_out, gf, tm):
    B, S, D = hp.shape
    tok = lambda width: pl.BlockSpec((1, tm, width), lambda b, i: (b, i, 0))
    return pl.pallas_call(
        _cd_out_kernel, grid=(B, S // tm),
        in_specs=[tok(cg.shape[2]), tok(a.shape[2]), tok(D), _const_spec(w_out.shape),
                  _const_spec((1, D))],
        out_specs=tok(D), out_shape=jax.ShapeDtypeStruct((B, S, D), F32),
        compiler_params=_params(("arbitrary", "arbitrary")), name="cd_out",
    )(cg, a, hp, w_out, gf)


QD_WIDTH = 2 * LANES


def _prep_absorb(w_ukv):
    w_uk = w_ukv[..., :MLA_NOPE]
    to_lat = jnp.transpose(w_uk, (1, 2, 0))
    top = jnp.concatenate([to_lat, jnp.zeros((MLA_HEADS, MLA_NOPE, HEAD_PAD), w_ukv.dtype)], axis=2)
    eye = jnp.eye(HEAD_PAD, dtype=w_ukv.dtype)[MLA_NOPE:]
    bot = jnp.concatenate([jnp.zeros((HEAD_PAD - MLA_NOPE, MLA_KV_RANK), w_ukv.dtype), eye], axis=1)
    bot = jnp.broadcast_to(bot[None], (MLA_HEADS,) + bot.shape)
    return jnp.concatenate([top, bot], axis=1).astype(BF16)


def _ab_in_sample_kernel(x_ref, c_ref, g_ref, wa_ref, wb_ref, qg_ref, kvg_ref, wuq_ref, absorb_ref,
                         lng_ref, lnb_ref, w00_ref, b0_ref,
                         qd_out, ckv_out, kr128_out, kr_out, vln_out, ga_out, gmb_out):
    c = c_ref[...]
    q_raw, ckv, kr128, zb = _ab_front(x_ref[...], c, g_ref[...], wa_ref[...], wb_ref[...],
                                      qg_ref[...], kvg_ref[...], wuq_ref[...])
    ckv_out[...] = ckv
    kr128_out[...] = kr128
    kr_out[...] = kr128[:, MLA_NOPE:MLA_NOPE + MLA_ROPE]
    for h in range(MLA_HEADS):
        qh = _rope_mix(q_raw[:, h * HEAD_PAD:(h + 1) * HEAD_PAD], c[0], c[1]).astype(BF16)
        qd_out[:, h * QD_WIDTH:(h + 1) * QD_WIDTH] = _dot(qh, absorb_ref[h]).astype(BF16)
    w = GM_WIDTH
    ga_out[...] = _silu(zb[:, :w]).astype(BF16)
    v_ln = _layernorm(zb[:, 2 * w:3 * w], lng_ref[...], lnb_ref[...])
    vln_out[...] = v_ln
    s = v_ln * w00_ref[...] + b0_ref[...]
    gmb_out[...] = (zb[:, w:2 * w] * s * _silu(zb[:, 3 * w:])).astype(BF16)


def _ab_in_sample(x, tables, norm_g, w_a, w_b, qg, kvg, w_uq_pad, absorb, ln_g, ln_b, w00, b0):
    n = x.shape[0]
    out_shape = [
        jax.ShapeDtypeStruct((n, MLA_HEADS * QD_WIDTH), BF16),
        jax.ShapeDtypeStruct((n, MLA_KV_RANK), F32), jax.ShapeDtypeStruct((n, HEAD_PAD), F32),
        jax.ShapeDtypeStruct((n, MLA_ROPE), F32), jax.ShapeDtypeStruct((n, GM_WIDTH), F32),
        jax.ShapeDtypeStruct((n, MLA_WIDTH), BF16), jax.ShapeDtypeStruct((n, GM_WIDTH), BF16)]
    return pl.pallas_call(
        _ab_in_sample_kernel, out_shape=out_shape,
        compiler_params=pltpu.CompilerParams(vmem_limit_bytes=VMEM_LIMIT), name="ab_in_sample",
    )(x, tables, norm_g, w_a, w_b, qg, kvg, w_uq_pad, absorb, ln_g, ln_b, w00, b0)


MLA_DECODE_CHUNK_PAGES = 16


def _mla_decode_kernel(pt_ref, qd_ref, cnew_ref, rnew_ref, ckv_hbm, kr_hbm, o_ref,
                       ckv_buf, kr_buf, s_ref, sem, *, layer):
    b = pl.program_id(0)
    nb = pl.num_programs(0)
    n_pages, page = ckv_buf.shape[1], ckv_buf.shape[2]
    slot = b % 2

    def page_copies(seq, slot_, p):
        phys = pt_ref[seq, p]
        lanes = pl.ds(pl.multiple_of(p * page, page), page)
        return (pltpu.make_async_copy(ckv_hbm.at[layer, phys], ckv_buf.at[slot_, p], sem.at[0, slot_]),
                pltpu.make_async_copy(kr_hbm.at[layer, phys], kr_buf.at[slot_, :, lanes], sem.at[1, slot_]))

    def fetch(seq, slot_):
        def body(p, carry):
            for cp in page_copies(seq, slot_, p):
                cp.start()
            return carry
        lax.fori_loop(0, n_pages, body, 0, unroll=4)

    @pl.when(b == 0)
    def _():
        fetch(0, 0)

    @pl.when(b + 1 < nb)
    def _():
        fetch(b + 1, 1 - slot)

    def wait_body(p, carry):
        for cp in page_copies(b, slot, p):
            cp.wait()
        return carry
    lax.fori_loop(0, n_pages, wait_body, 0, unroll=4)

    qd = qd_ref[0]
    q_lat = qd[:, :MLA_KV_RANK]
    q_pad = qd[:, MLA_KV_RANK:]
    q_rope = q_pad[:, MLA_NOPE:MLA_NOPE + MLA_ROPE]
    cp_ = MLA_DECODE_CHUNK_PAGES
    rows = cp_ * page
    n_chunks = n_pages // cp_

    def load_latent(ci):
        pages = pl.ds(pl.multiple_of(ci * cp_, cp_), cp_)
        return ckv_buf[slot, pages].reshape(rows, MLA_KV_RANK).astype(BF16)

    def score_body(ci, m):
        pos = pl.ds(pl.multiple_of(ci * rows, rows), rows)
        r = kr_buf[slot, :, pos].astype(BF16)
        s = _dot_nt(q_lat, load_latent(ci)) + _dot(q_rope, r)
        s_ref[:, pos] = s
        return jnp.maximum(m, jnp.max(s, axis=1, keepdims=True))

    c_new = cnew_ref[0]
    s_new = (jnp.sum(q_lat.astype(F32) * c_new, axis=1, keepdims=True)
             + jnp.sum(q_pad.astype(F32) * rnew_ref[0], axis=1, keepdims=True))
    m = lax.fori_loop(0, n_chunks, score_body, s_new, unroll=True)

    def value_body(ci, carry):
        l, o = carry
        p = jnp.exp2(s_ref[:, pl.ds(pl.multiple_of(ci * rows, rows), rows)] - m)
        return l + jnp.sum(p, axis=1, keepdims=True), o + _dot(p.astype(BF16), load_latent(ci))

    p_new = jnp.exp2(s_new - m)
    l, o = lax.fori_loop(0, n_chunks, value_body, (p_new, p_new * c_new), unroll=True)
    o_ref[0] = o / l


def _mla_decode(page_table, qd, ckv_new, kr128_new, cache_ckv, cache_kr_t, layer):
    n, n_pages = page_table.shape
    page = cache_ckv.shape[2]
    assert n_pages % MLA_DECODE_CHUNK_PAGES == 0 and page % LANES == 0
    grid_spec = pltpu.PrefetchScalarGridSpec(
        num_scalar_prefetch=1, grid=(n,),
        in_specs=[pl.BlockSpec((1, MLA_HEADS, QD_WIDTH), lambda b, pt: (b, 0, 0)),
                  pl.BlockSpec((1, 1, MLA_KV_RANK), lambda b, pt: (b, 0, 0)),
                  pl.BlockSpec((1, 1, HEAD_PAD), lambda b, pt: (b, 0, 0)),
                  pl.BlockSpec(memory_space=pl.ANY), pl.BlockSpec(memory_space=pl.ANY)],
        out_specs=pl.BlockSpec((1, MLA_HEADS, MLA_KV_RANK), lambda b, pt: (b, 0, 0)),
        scratch_shapes=[pltpu.VMEM((2, n_pages, page, MLA_KV_RANK), F32),
                        pltpu.VMEM((2, MLA_ROPE, n_pages * page), F32),
                        pltpu.VMEM((MLA_HEADS, n_pages * page), F32),
                        pltpu.SemaphoreType.DMA((2, 2))])
    return pl.pallas_call(
        functools.partial(_mla_decode_kernel, layer=layer), grid_spec=grid_spec,
        out_shape=jax.ShapeDtypeStruct((n, MLA_HEADS, MLA_KV_RANK), F32),
        compiler_params=_params(("arbitrary",)), name="mla_decode",
    )(page_table, qd.reshape(n, MLA_HEADS, QD_WIDTH), ckv_new.reshape(n, 1, MLA_KV_RANK),
      kr128_new.reshape(n, 1, HEAD_PAD), cache_ckv, cache_kr_t)


def _mid_sample_kernel(ol_ref, wuv_ref, ga_ref, gm_ref, x_ref, wo_ref, g2_ref, w2_ref, cw_ref,
                       s0_ref, s1_ref,
                       hp_out, cg_out, q_out, k_out, v_out, mg_out, pre_out):
    w = SC_WIDTH
    half = wo_ref.shape[0] // 2
    att = _dot(ol_ref[...].astype(BF16), wuv_ref[...])
    a = (att * ga_ref[...].astype(F32)).astype(BF16)
    hp = x_ref[...] + _dot(a, wo_ref[:half, :]) + _dot(gm_ref[...], wo_ref[half:, :])
    hp_out[...] = hp
    z = _cd_front(hp, g2_ref[...], w2_ref[...])
    pre = z[:, 2 * w:3 * w] * z[:, :w]
    pre_out[...] = pre
    cw = cw_ref[...]
    conv = cw[0:1] * s0_ref[...] + cw[1:2] * s1_ref[...] + cw[2:3] * pre
    cg_out[...] = (z[:, w:2 * w] * conv * _silu(z[:, 3 * w:4 * w])).astype(BF16)
    q_out[...] = z[:, 4 * w:5 * w] * MOBA_SCALE
    k_out[...] = z[:, 5 * w:6 * w]
    v_out[...] = z[:, 6 * w:7 * w]
    mg_out[...] = _silu(z[:, 7 * w:])


def _mid_sample(o_lat, w_uv_bd, ga, gm, x, w_out, g2, w2, conv_w, s0, s1):
    n, D = x.shape
    w = SC_WIDTH
    ff = lambda: jax.ShapeDtypeStruct((n, w), F32)
    bf = lambda: jax.ShapeDtypeStruct((n, w), BF16)
    out_shape = [jax.ShapeDtypeStruct((n, D), F32), bf(), ff(), ff(), ff(), ff(), ff()]
    return pl.pallas_call(
        _mid_sample_kernel, out_shape=out_shape,
        compiler_params=pltpu.CompilerParams(vmem_limit_bytes=VMEM_LIMIT), name="mid_sample",
    )(o_lat, w_uv_bd, ga, gm, x, w_out, g2, w2, conv_w, s0, s1)


KSUM_CHUNK_PAGES = 8
KSUM_RING = 3


def _moba_ksum_kernel(pt_ref, k_hbm, o_ref, buf, sem, *, layer, pages_per_block):
    b = pl.program_id(0)
    nb = pl.num_programs(0)
    n_pages = pt_ref.shape[1]
    cp_ = KSUM_CHUNK_PAGES
    n_chunks = n_pages // cp_
    total = nb * n_chunks

    def copies(g, slot):
        seq = g // n_chunks
        c = g - seq * n_chunks
        return [pltpu.make_async_copy(k_hbm.at[layer, pt_ref[seq, c * cp_ + i]], buf.at[slot, i],
                                      sem.at[slot]) for i in range(cp_)]

    def start(g):
        for cp in copies(g, g % KSUM_RING):
            cp.start()

    @pl.when(b == 0)
    def _():
        for g in range(KSUM_RING - 1):
            start(g)

    def body(c, carry):
        g = b * n_chunks + c
        slot = g % KSUM_RING

        @pl.when(g + KSUM_RING - 1 < total)
        def _():
            start(g + KSUM_RING - 1)

        for cp in copies(g, slot):
            cp.wait()

        @pl.when(c == 0)
        def _():
            o_ref[0] = jnp.zeros(o_ref.shape[1:], F32)

        lane = lax.broadcasted_iota(jnp.int32, o_ref.shape[1:], 1)
        bpc = cp_ // pages_per_block
        for j in range(bpc):
            x = buf[slot, j * pages_per_block]
            for r in range(1, pages_per_block):
                x = x + buf[slot, j * pages_per_block + r]
            blk_sum = jnp.sum(x.reshape(o_ref.shape[1], x.shape[-1]), axis=1, keepdims=True)
            o_ref[0] = jnp.where(lane == c * bpc + j, blk_sum, o_ref[0])
        return carry

    lax.fori_loop(0, n_chunks, body, 0)


def _moba_ksum(page_table, cache_k_t, layer):
    n, n_pages = page_table.shape
    _, _, H, Dh, page = cache_k_t.shape
    ppb = MOBA_BLOCK // page
    assert n_pages % KSUM_CHUNK_PAGES == 0 and KSUM_CHUNK_PAGES % ppb == 0
    assert n * (n_pages // KSUM_CHUNK_PAGES) >= KSUM_RING
    assert n_pages // ppb <= LANES and page % LANES == 0
    grid_spec = pltpu.PrefetchScalarGridSpec(
        num_scalar_prefetch=1, grid=(n,),
        in_specs=[pl.BlockSpec(memory_space=pl.ANY)],
        out_specs=pl.BlockSpec((1, H * Dh, LANES), lambda b, pt: (b, 0, 0)),
        scratch_shapes=[pltpu.VMEM((KSUM_RING, KSUM_CHUNK_PAGES, H, Dh, page), F32),
                        pltpu.SemaphoreType.DMA((KSUM_RING,))])
    return pl.pallas_call(
        functools.partial(_moba_ksum_kernel, layer=layer, pages_per_block=ppb), grid_spec=grid_spec,
        out_shape=jax.ShapeDtypeStruct((n, H * Dh, LANES), F32),
        compiler_params=_params(("arbitrary",)), name="moba_ksum",
    )(page_table, cache_k_t)


def _moba_select_kernel(ks_ref, q_ref, idx_out, *, n_blocks):
    H = idx_out.shape[1]
    W = q_ref.shape[2]
    row = lax.broadcasted_iota(jnp.int32, (H, W), 0)
    col = lax.broadcasted_iota(jnp.int32, (H, W), 1)
    dh = W // H
    assert dh & (dh - 1) == 0
    q_heads = jnp.where(lax.shift_right_logical(col, dh.bit_length() - 1) == row, q_ref[0], 0.0)
    gate = jnp.dot(q_heads, ks_ref[0], precision=lax.Precision.HIGHEST,
                   preferred_element_type=F32)
    lane = lax.broadcasted_iota(jnp.int32, gate.shape, 1)
    gate = jnp.where(lane < n_blocks, gate, -jnp.inf)
    idx = jnp.zeros(gate.shape, jnp.int32)
    for t in range(MOBA_TOPK):
        mx = jnp.max(gate, axis=1, keepdims=True)
        first = jnp.min(jnp.where(gate == mx, lane, LANES), axis=1, keepdims=True)
        idx = jnp.where(lane == t, first, idx)
        gate = jnp.where(lane == first, -jnp.inf, gate)
    idx_out[0] = idx


def _moba_select(ksum, q, n_blocks):
    n, W, _ = ksum.shape
    return pl.pallas_call(
        functools.partial(_moba_select_kernel, n_blocks=n_blocks), grid=(n,),
        in_specs=[pl.BlockSpec((1, W, LANES), lambda b: (b, 0, 0)),
                  pl.BlockSpec((1, 1, W), lambda b: (b, 0, 0))],
        out_specs=pl.BlockSpec((1, MOBA_HEADS, LANES), lambda b: (b, 0, 0)),
        out_shape=jax.ShapeDtypeStruct((n, MOBA_HEADS, LANES), jnp.int32),
        compiler_params=_params(("arbitrary",)), name="moba_select",
    )(ksum, q)


def _moba_decode_kernel(pt_ref, idx_ref, q_ref, kn_ref, vn_ref, mg_ref, k_hbm, v_hbm, o_ref,
                        kbuf, vbuf, sem, *, layer, pages_per_block):
    b = pl.program_id(0)
    nb = pl.num_programs(0)
    H = q_ref.shape[1]
    slot = b % 2
    page = kbuf.shape[3] // (MOBA_TOPK * pages_per_block)

    def copies(seq, slot_):
        out = []
        for h in range(H):
            for t in range(MOBA_TOPK):
                blk = idx_ref[seq, h * MOBA_TOPK + t]
                for r in range(pages_per_block):
                    phys = pt_ref[seq, blk * pages_per_block + r]
                    lanes = pl.ds((t * pages_per_block + r) * page, page)
                    out.append(pltpu.make_async_copy(k_hbm.at[layer, phys, h],
                                                     kbuf.at[slot_, h, :, lanes], sem.at[0, slot_]))
                    out.append(pltpu.make_async_copy(v_hbm.at[layer, phys, h],
                                                     vbuf.at[slot_, h, :, lanes], sem.at[1, slot_]))
        return out

    @pl.when(b == 0)
    def _():
        for cp in copies(0, 0):
            cp.start()

    @pl.when(b + 1 < nb)
    def _():
        for cp in copies(b + 1, 1 - slot):
            cp.start()

    for cp in copies(b, slot):
        cp.wait()

    q = q_ref[0]
    qb = q.astype(BF16)
    k_new, v_new = kn_ref[0], vn_ref[0]
    s_new = jnp.sum(q * k_new, axis=1, keepdims=True)
    row = lax.broadcasted_iota(jnp.int32, q.shape, 0)
    out = jnp.zeros(q.shape, F32)
    for h in range(H):
        kh = kbuf[slot, h].astype(BF16)
        vh = vbuf[slot, h].astype(BF16)
        s = _dot(qb, kh)
        m = jnp.maximum(jnp.max(s, axis=1, keepdims=True), s_new)
        p = jnp.exp(s - m)
        p_new = jnp.exp(s_new - m)
        l = jnp.sum(p, axis=1, keepdims=True) + p_new
        o = (_dot_nt(p.astype(BF16), vh) + p_new * v_new) / l
        out = jnp.where(row == h, o, out)
    o_ref[0] = out * mg_ref[0]


def _moba_decode(page_table, idx, q, k_new, v_new, mg, cache_k_t, cache_v_t, layer):
    n, n_pages = page_table.shape
    _, _, H, Dh, page = cache_k_t.shape
    ppb = MOBA_BLOCK // page
    tok = pl.BlockSpec((1, H, Dh), lambda b, pt, ix: (b, 0, 0))
    grid_spec = pltpu.PrefetchScalarGridSpec(
        num_scalar_prefetch=2, grid=(n,),
        in_specs=[tok, tok, tok, tok, pl.BlockSpec(memory_space=pl.ANY),
                  pl.BlockSpec(memory_space=pl.ANY)],
        out_specs=tok,
        scratch_shapes=[pltpu.VMEM((2, H, Dh, MOBA_TOPK * ppb * page), F32),
                        pltpu.VMEM((2, H, Dh, MOBA_TOPK * ppb * page), F32),
                        pltpu.SemaphoreType.DMA((2, 2))])
    return pl.pallas_call(
        functools.partial(_moba_decode_kernel, layer=layer, pages_per_block=ppb),
        grid_spec=grid_spec, out_shape=jax.ShapeDtypeStruct((n, H, Dh), F32),
        compiler_params=_params(("arbitrary",)), name="moba_decode",
    )(page_table, idx, q, k_new, v_new, mg, cache_k_t, cache_v_t)


PROJ_TILE = 512
ATTN_TILE = 512


def kernel(x_prompt, x_sample, cache_mla_ckv, cache_mla_krope, cache_moba_k, cache_moba_v, state_conv,
           page_table, ab_norm_g, ab_w_in, ab_q_norm_g, ab_kv_norm_g, ab_w_uq, ab_w_ukv, ab_gm_ln_g,
           ab_gm_ln_b, ab_gm_ws, ab_gm_bs, ab_w_out, cd_norm_g, cd_w_in, cd_conv_w, cd_w_out,
           final_norm_g):
    B, S, D = x_prompt.shape
    DB, T, _ = x_sample.shape
    n_pages = page_table.shape[1]
    page = cache_mla_ckv.shape[2]
    past_len = n_pages * page
    assert T == 1 and ab_w_in.shape[0] == 1 and cd_w_in.shape[0] == 1
    assert past_len % MOBA_BLOCK == 0 and past_len // MOBA_BLOCK >= MOBA_TOPK
    assert S % PROJ_TILE == 0 and S % ATTN_TILE == 0 and PROJ_TILE % MOBA_BLOCK == 0
    li = 0
    row = lambda v: v.reshape(1, -1)

    w_a, w_b, w_uq_pad, w_uk_pad, w_uv_flat = _prep_ab_weights(ab_w_in[li], ab_w_uq[li], ab_w_ukv[li])
    absorb = _prep_absorb(ab_w_ukv[li])
    w_uv = ab_w_ukv[li][..., MLA_NOPE:]
    w_uv_bd = (jnp.eye(MLA_HEADS, dtype=F32)[:, None, :, None] * jnp.transpose(w_uv, (1, 0, 2))[:, :, None, :])
    w_uv_bd = w_uv_bd.reshape(MLA_HEADS * MLA_KV_RANK, MLA_WIDTH).astype(BF16)
    ws = ab_gm_ws[li]
    bs_exp = jnp.repeat(ab_gm_bs[li].T, GM_GROUP_DIM, axis=1)
    w00 = jnp.repeat(ws[:, 0, 0], GM_GROUP_DIM).reshape(1, GM_WIDTH)
    b0 = bs_exp[0:1]
    w_out0 = ab_w_out[li].astype(BF16)
    w2 = cd_w_in[li].astype(BF16)
    w_out1 = cd_w_out[li].astype(BF16)
    conv_w = cd_conv_w[li]
    tab_p = _rope_tables(jnp.arange(S, dtype=jnp.int32))
    tab_s = _rope_tables(jnp.full((DB,), past_len, jnp.int32))
    g0, qg, kvg = row(ab_norm_g[li]), row(ab_q_norm_g[li]), row(ab_kv_norm_g[li])
    ln_g, ln_b = row(ab_gm_ln_g[li]), row(ab_gm_ln_b[li])
    g2, gf = row(cd_norm_g[li]), row(final_norm_g)

    q, k, v, ckv_p, kr_p, ga, gmb = _ab_in_prompt(
        x_prompt, tab_p, g0, w_a, w_b, qg, kvg, w_uq_pad, w_uk_pad, w_uv_flat, ln_g, ln_b, ws, bs_exp,
        PROJ_TILE)
    a0 = _mla_attn_prompt(q, k, v, ga, ATTN_TILE)
    hp, cg, q1, k1b, v1b, mg, k1, v1, kmean, conv_p = _mid_prompt(
        a0, gmb, x_prompt, w_out0, g2, w2, conv_w, PROJ_TILE)
    nb = S // MOBA_BLOCK
    assert nb <= LANES
    nb_pad = -(-nb // SUBLANES) * SUBLANES
    kmean_pad = jnp.pad(kmean.reshape(B, nb, MOBA_WIDTH), ((0, 0), (0, nb_pad - nb), (0, 0)))
    a1 = _moba_attn_prompt(q1, k1b, v1b, kmean_pad, mg)
    y_prompt = _cd_out(cg, a1, hp, w_out1, gf, PROJ_TILE)

    xs = x_sample.reshape(DB, D)
    qd, ckv_s, kr128_s, kr_s, vln_s, ga_s, gmb_s = _ab_in_sample(
        xs, tab_s, g0, w_a, w_b, qg, kvg, w_uq_pad, absorb, ln_g, ln_b, w00, b0)
    cache_kr_t = jnp.transpose(cache_mla_krope, (0, 1, 3, 2))
    cache_k_t = jnp.transpose(cache_moba_k, (0, 1, 3, 4, 2))
    cache_v_t = jnp.transpose(cache_moba_v, (0, 1, 3, 4, 2))
    o_lat = _mla_decode(page_table, qd, ckv_s, kr128_s, cache_mla_ckv, cache_kr_t, li)
    hs, cg_s, q_s, k_s, v_s, mg_s, pre_s = _mid_sample(
        o_lat.reshape(DB, MLA_HEADS * MLA_KV_RANK), w_uv_bd, ga_s, gmb_s, xs, w_out0, g2, w2, conv_w,
        state_conv[li, :, 0], state_conv[li, :, 1])
    ksum = _moba_ksum(page_table, cache_k_t, li)
    heads = lambda t: t.reshape(DB, MOBA_HEADS, MOBA_HEAD_DIM)
    idx = _moba_select(ksum, q_s.reshape(DB, 1, MOBA_WIDTH), past_len // MOBA_BLOCK)
    idx = idx[:, :, :MOBA_TOPK].reshape(DB, MOBA_HEADS * MOBA_TOPK)
    a1_s = _moba_decode(page_table, idx, heads(q_s), heads(k_s), heads(v_s),
                        heads(mg_s), cache_k_t, cache_v_t, li)
    y_sample = _cd_out(cg_s[None], a1_s.reshape(1, DB, MOBA_WIDTH).astype(BF16), hs[None], w_out1, gf, DB)

    return (y_prompt, y_sample.reshape(DB, 1, D),
            ckv_p[None], kr_p[None],
            ckv_s.reshape(1, DB, 1, MLA_KV_RANK), kr_s.reshape(1, DB, 1, MLA_ROPE),
            vln_s.reshape(1, DB, 1, GM_WIDTH),
            conv_p[None], jnp.stack([state_conv[li, :, 1], pre_s], axis=1)[None],
            k1.reshape(1, B, S, MOBA_HEADS, MOBA_HEAD_DIM), v1.reshape(1, B, S, MOBA_HEADS, MOBA_HEAD_DIM),
            k_s.reshape(1, DB, 1, MOBA_HEADS, MOBA_HEAD_DIM), v_s.reshape(1, DB, 1, MOBA_HEADS, MOBA_HEAD_DIM))
```

```python
import functools

import jax
import jax.numpy as jnp
import numpy as np
from jax import lax
from jax.experimental import pallas as pl
from jax.experimental.pallas import tpu as pltpu

F32 = jnp.float32
BF16 = jnp.bfloat16

MLA_HEADS = 8
MLA_NOPE = 64
MLA_ROPE = 32
MLA_V = 64
MLA_Q_RANK = 256
MLA_KV_RANK = 128
MLA_WIDTH = MLA_HEADS * MLA_V
MLA_SCALE = (MLA_NOPE + MLA_ROPE) ** -0.5
ROPE_THETA = 10000.0
GM_GROUPS = 8
GM_WIDTH = 512
GM_GROUP_DIM = GM_WIDTH // GM_GROUPS
GM_CHUNK = 128
SC_WIDTH = 512
CONV_LEN = 3
MOBA_HEADS = 8
MOBA_HEAD_DIM = 64
MOBA_WIDTH = MOBA_HEADS * MOBA_HEAD_DIM
MOBA_BLOCK = 256
MOBA_TOPK = 3
MOBA_SCALE = MOBA_HEAD_DIM ** -0.5
NORM_EPS = 1e-6

LANES = 128
SUBLANES = 8
HEAD_PAD = 128
ROLL_SPARE_TO_ROPE = LANES - MLA_ROPE
M_INIT = -1e30
MASKED = -2e30
LOG2E = 1.4426950408889634
VMEM_LIMIT = 56 * 1024 * 1024


def _dot(a, b):
    return jnp.dot(a, b, preferred_element_type=F32)


def _dot_nt(a, b, precision=None):
    return lax.dot_general(a, b, (((1,), (1,)), ((), ())), precision=precision,
                           preferred_element_type=F32)


def _rms(x, g):
    return x * lax.rsqrt(jnp.mean(x * x, axis=-1, keepdims=True) + NORM_EPS) * g


def _layernorm(x, g, b):
    mu = jnp.mean(x, axis=-1, keepdims=True)
    d = x - mu
    var = jnp.mean(d * d, axis=-1, keepdims=True)
    return d * lax.rsqrt(var + NORM_EPS) * g + b


def _silu(x):
    return x * jax.nn.sigmoid(x)


def _rope_mix(raw, c_keep, c_rot):
    return raw * c_keep + pltpu.roll(raw, ROLL_SPARE_TO_ROPE, 1) * c_rot


def _const_spec(shape):
    nd = len(shape)
    return pl.BlockSpec(shape, lambda *_: (0,) * nd)


def _params(sem):
    return pltpu.CompilerParams(dimension_semantics=sem, vmem_limit_bytes=VMEM_LIMIT)


def _rot_half_cols(w):
    half = w.shape[-1] // 2
    return jnp.concatenate([-w[..., half:], w[..., :half]], axis=-1)


def _prep_ab_weights(w_in, w_uq, w_ukv):
    d = w_in.shape[0]
    o = np.cumsum((0, MLA_Q_RANK, MLA_KV_RANK, MLA_ROPE))
    w_cq, w_ckv, w_kr = (w_in[:, o[i]:o[i + 1]] for i in range(3))
    w_kr_blk = jnp.concatenate(
        [jnp.zeros((d, MLA_NOPE), w_in.dtype), w_kr, _rot_half_cols(w_kr)], axis=1)
    w_a = jnp.concatenate([w_cq, w_ckv, w_kr_blk], axis=1).astype(BF16)
    w_b = w_in[:, o[3]:].astype(BF16)
    uq_nope, uq_rope = w_uq[..., :MLA_NOPE], w_uq[..., MLA_NOPE:]
    w_uq_pad = jnp.concatenate([uq_nope, uq_rope, _rot_half_cols(uq_rope)], axis=-1)
    w_uq_pad = w_uq_pad.reshape(MLA_Q_RANK, MLA_HEADS * HEAD_PAD).astype(BF16)
    w_uk, w_uv = w_ukv[..., :MLA_NOPE], w_ukv[..., MLA_NOPE:]
    w_uk_pad = jnp.concatenate([w_uk, jnp.zeros_like(w_uk)], axis=-1)
    w_uk_pad = w_uk_pad.reshape(MLA_KV_RANK, MLA_HEADS * HEAD_PAD).astype(BF16)
    w_uv_flat = w_uv.reshape(MLA_KV_RANK, MLA_WIDTH).astype(BF16)
    return w_a, w_b, w_uq_pad, w_uk_pad, w_uv_flat


def _rope_tables(pos):
    half = MLA_ROPE // 2
    inv = ROPE_THETA ** (-jnp.arange(half, dtype=F32) / half)
    ang = pos.astype(F32)[:, None] * inv[None, :]
    cos, sin = jnp.cos(ang), jnp.sin(ang)
    n = pos.shape[0]
    cos2 = jnp.concatenate([cos, cos], axis=1)
    sin2 = jnp.concatenate([sin, sin], axis=1)
    zeros_nope = jnp.zeros((n, MLA_NOPE), F32)
    ones_nope = jnp.ones((n, MLA_NOPE), F32)
    spare = jnp.zeros((n, HEAD_PAD - MLA_NOPE - MLA_ROPE), F32)
    q_keep = jnp.concatenate([ones_nope, cos2, spare], axis=1) * (MLA_SCALE * LOG2E)
    q_rot = jnp.concatenate([zeros_nope, sin2, spare], axis=1) * (MLA_SCALE * LOG2E)
    k_keep = jnp.concatenate([zeros_nope, cos2, spare], axis=1)
    k_rot = jnp.concatenate([zeros_nope, sin2, spare], axis=1)
    return jnp.stack([q_keep, q_rot, k_keep, k_rot])


def _ab_front(x, c, g, wa, wb, qg, kvg, wuq):
    h = _rms(x, g).astype(BF16)
    za = _dot(h, wa)
    zb = _dot(h, wb)
    cqn = _rms(za[:, :MLA_Q_RANK], qg).astype(BF16)
    q_raw = _dot(cqn, wuq)
    ckv = _rms(za[:, MLA_Q_RANK:MLA_Q_RANK + MLA_KV_RANK], kvg)
    kr128 = _rope_mix(za[:, MLA_Q_RANK + MLA_KV_RANK:], c[2], c[3])
    return q_raw, ckv, kr128, zb


def _ab_in_kernel(x_ref, c_ref, g_ref, wa_ref, wb_ref, qg_ref, kvg_ref, wuq_ref, wuk_ref, wuv_ref,
                  lng_ref, lnb_ref, ws_ref, bse_ref,
                  q_out, k_out, v_out, ckv_out, kr_out, ga_out, gmb_out):
    tm = x_ref.shape[1]
    c = c_ref[...]
    q_raw, ckv, kr128, zb = _ab_front(x_ref[0], c, g_ref[...], wa_ref[...], wb_ref[...],
                                      qg_ref[...], kvg_ref[...], wuq_ref[...])
    ckv_out[0] = ckv
    kr_out[0] = kr128[:, MLA_NOPE:MLA_NOPE + MLA_ROPE]
    ckv_b = ckv.astype(BF16)
    kn = _dot(ckv_b, wuk_ref[...])
    v_out[0] = _dot(ckv_b, wuv_ref[...]).astype(BF16)
    for h in range(MLA_HEADS):
        sl = slice(h * HEAD_PAD, (h + 1) * HEAD_PAD)
        q_out[0, :, sl] = _rope_mix(q_raw[:, sl], c[0], c[1]).astype(BF16)
        k_out[0, :, sl] = (kn[:, sl] + kr128).astype(BF16)
    w = GM_WIDTH
    ga_out[0] = _silu(zb[:, :w]).astype(BF16)
    u, v, g_gm = zb[:, w:2 * w], zb[:, 2 * w:3 * w], zb[:, 3 * w:]
    v_ln = _layernorm(v, lng_ref[...], lnb_ref[...]).astype(BF16)
    row = lax.broadcasted_iota(jnp.int32, (GM_CHUNK, GM_CHUNK), 0)
    col = lax.broadcasted_iota(jnp.int32, (GM_CHUNK, GM_CHUNK), 1)
    lane = lax.broadcasted_iota(jnp.int32, (GM_CHUNK, LANES), 1)
    ws_tril = [jnp.where(col <= row, ws_ref[gi], 0.0).astype(BF16) for gi in range(GM_GROUPS)]
    groups_per_vreg = LANES // GM_GROUP_DIM
    assert groups_per_vreg == 2
    for ci in range(tm // GM_CHUNK):
        rows = slice(ci * GM_CHUNK, (ci + 1) * GM_CHUNK)
        for j in range(w // LANES):
            cols = slice(j * LANES, (j + 1) * LANES)
            vb = v_ln[rows, cols]
            s_lo = _dot(ws_tril[2 * j], vb)
            s_hi = _dot(ws_tril[2 * j + 1], vb)
            s = jnp.where(lane < GM_GROUP_DIM, s_lo, s_hi) + bse_ref[:, cols]
            gm = u[rows, cols] * s * _silu(g_gm[rows, cols])
            gmb_out[0, rows, cols] = gm.astype(BF16)


def _ab_in_prompt(x, tables, norm_g, w_a, w_b, qg, kvg, w_uq_pad, w_uk_pad, w_uv_flat,
                  ln_g, ln_b, ws, bs_exp, tm):
    B, S, D = x.shape
    grid = (S // tm, B)
    tok = lambda w: pl.BlockSpec((1, tm, w), lambda i, b: (b, i, 0))
    in_specs = [
        tok(D),
        pl.BlockSpec((4, tm, LANES), lambda i, b: (0, i, 0)),
        _const_spec((1, D)), _const_spec(w_a.shape), _const_spec(w_b.shape),
        _const_spec((1, MLA_Q_RANK)), _const_spec((1, MLA_KV_RANK)),
        _const_spec(w_uq_pad.shape), _const_spec(w_uk_pad.shape), _const_spec(w_uv_flat.shape),
        _const_spec((1, GM_WIDTH)), _const_spec((1, GM_WIDTH)),
        _const_spec(ws.shape), _const_spec(bs_exp.shape),
    ]
    hp = MLA_HEADS * HEAD_PAD
    out_shape = [
        jax.ShapeDtypeStruct((B, S, hp), BF16), jax.ShapeDtypeStruct((B, S, hp), BF16),
        jax.ShapeDtypeStruct((B, S, MLA_WIDTH), BF16),
        jax.ShapeDtypeStruct((B, S, MLA_KV_RANK), F32), jax.ShapeDtypeStruct((B, S, MLA_ROPE), F32),
        jax.ShapeDtypeStruct((B, S, MLA_WIDTH), BF16), jax.ShapeDtypeStruct((B, S, GM_WIDTH), BF16),
    ]
    out_specs = [tok(hp), tok(hp), tok(MLA_WIDTH), tok(MLA_KV_RANK), tok(MLA_ROPE),
                 tok(MLA_WIDTH), tok(GM_WIDTH)]
    return pl.pallas_call(
        _ab_in_kernel, grid=grid, in_specs=in_specs, out_specs=out_specs, out_shape=out_shape,
        compiler_params=_params(("arbitrary", "arbitrary")), name="ab_in_prompt",
    )(x, tables, norm_g, w_a, w_b, qg, kvg, w_uq_pad, w_uk_pad, w_uv_flat, ln_g, ln_b, ws, bs_exp)


def _tri_tables(n):
    qi, ki = np.tril_indices(n)
    return jnp.asarray(qi, jnp.int32), jnp.asarray(ki, jnp.int32)


def _with_ones(v_pair):
    return jnp.concatenate([v_pair, jnp.ones(v_pair.shape, v_pair.dtype)], axis=1)


def _softmax_step(s, v_ones, h, m_ref, l_ref, acc_ref):
    tk = s.shape[1]
    m_prev = m_ref[h]
    m_next = jnp.maximum(m_prev, jnp.max(s, axis=1, keepdims=True))
    p = jnp.exp2(s - jnp.concatenate([m_next] * (tk // LANES), axis=1))
    alpha = jnp.exp2(m_prev - m_next)
    pv = _dot(p.astype(BF16), v_ones)
    l_ref[h] = alpha * l_ref[h] + pv[:, LANES:]
    m_ref[h] = m_next
    acc_ref[h] = acc_ref[h] * alpha + pv[:, :LANES]


def _attn_init(m_ref, l_ref, acc_ref):
    m_ref[...] = jnp.full(m_ref.shape, M_INIT, F32)
    l_ref[...] = jnp.zeros(l_ref.shape, F32)
    acc_ref[...] = jnp.zeros(acc_ref.shape, F32)


def _attn_finish(gate_ref, o_ref, l_ref, acc_ref, n_heads):
    tq = acc_ref.shape[1]
    lane = lax.broadcasted_iota(jnp.int32, (tq, LANES), 1)
    for j in range(n_heads // 2):
        lo = acc_ref[2 * j] / l_ref[2 * j]
        hi = acc_ref[2 * j + 1] / l_ref[2 * j + 1]
        cols = slice(j * LANES, (j + 1) * LANES)
        o = jnp.where(lane < LANES // 2, lo, hi) * gate_ref[0, :, cols].astype(F32)
        o_ref[0, :, cols] = o.astype(o_ref.dtype)


KSUM_CHUNK_PAGES = 8
KSUM_STEP_RING = 3


def _mla_attn_kernel(qi_ref, ki_ref, pt_ref, q_ref, k_ref, v_ref, ga_ref, kc_hbm, o_ref, ks_hbm,
                     m_ref, l_ref, acc_ref, kbuf, racc, ksem, osem,
                     *, layer, pages_per_block, chunks_per_step):
    t = q_ref.shape[1]
    pair = pl.program_id(1)
    qi, ki = qi_ref[pair], ki_ref[pair]
    step_id = pl.program_id(0) * pl.num_programs(1) + pair
    n_seq, n_pages = pt_ref.shape
    cpp = KSUM_CHUNK_PAGES
    cps = chunks_per_step
    chunks_per_seq = n_pages // cpp
    total_steps = (n_seq * chunks_per_seq) // cps
    steps_per_seq = chunks_per_seq // cps
    page = kbuf.shape[-1]
    width = racc.shape[0]

    def step_copies(s):
        seq = s // steps_per_seq
        first_page = (s - seq * steps_per_seq) * (cps * cpp)
        slot = s % KSUM_STEP_RING
        return [pltpu.make_async_copy(kc_hbm.at[layer, pt_ref[seq, first_page + i]],
                                      kbuf.at[slot, i], ksem.at[slot]) for i in range(cps * cpp)]

    def start_if_valid(s):
        @pl.when(s < total_steps)
        def _():
            for cp in step_copies(s):
                cp.start()

    @pl.when(step_id == 0)
    def _():
        racc[...] = jnp.zeros(racc.shape, F32)
        for s in range(KSUM_STEP_RING - 1):
            start_if_valid(s)

    start_if_valid(step_id + KSUM_STEP_RING - 1)

    @pl.when(step_id < total_steps)
    def _():
        for cp in step_copies(step_id):
            cp.wait()

    seq = step_id // steps_per_seq
    first_chunk = (step_id - seq * steps_per_seq) * cps
    has_cache_work = step_id < total_steps

    def cache_block_sums():
        slot = step_id % KSUM_STEP_RING
        lane = lax.broadcasted_iota(jnp.int32, (width, LANES), 1)
        r = jnp.where(first_chunk == 0, 0.0, racc[...])
        blocks = cps * cpp // pages_per_block
        first_block = jnp.where(has_cache_work, first_chunk * (cpp // pages_per_block), -blocks)
        for j in range(blocks):
            x = kbuf[slot, j * pages_per_block]
            for r_ in range(1, pages_per_block):
                x = x + kbuf[slot, j * pages_per_block + r_]
            blk_sum = jnp.sum(x.reshape(width, page), axis=1, keepdims=True)
            r = jnp.where(lane == first_block + j, blk_sum, r)
        racc[...] = r

    @pl.when(ki == 0)
    def _():
        _attn_init(m_ref, l_ref, acc_ref)

    def step(diagonal):
        cache_block_sums()
        if diagonal:
            row = lax.broadcasted_iota(jnp.int32, (t, t), 0)
            col = lax.broadcasted_iota(jnp.int32, (t, t), 1)
            keep = col <= row
        for h in range(MLA_HEADS):
            sl = slice(h * HEAD_PAD, (h + 1) * HEAD_PAD)
            s = _dot_nt(q_ref[0, :, sl], k_ref[0, :, sl])
            if diagonal:
                s = jnp.where(keep, s, MASKED)
            vp = v_ref[0, :, (h // 2) * LANES:(h // 2 + 1) * LANES]
            _softmax_step(s, _with_ones(vp), h, m_ref, l_ref, acc_ref)

    @pl.when(ki < qi)
    def _():
        step(False)

    @pl.when(ki == qi)
    def _():
        step(True)
        _attn_finish(ga_ref, o_ref, l_ref, acc_ref, MLA_HEADS)

    @pl.when(has_cache_work & (first_chunk + cps == chunks_per_seq))
    def _():
        out = pltpu.make_async_copy(racc, ks_hbm.at[seq], osem)
        out.start()
        out.wait()


def _mla_attn_prompt(q, k, v, ga, t, page_table, cache_k_t, layer):
    B, S, _ = q.shape
    n = S // t
    qi_tbl, ki_tbl = _tri_tables(n)
    n_steps = B * qi_tbl.shape[0]
    n_seq, n_pages = page_table.shape
    _, _, H, Dh, page = cache_k_t.shape
    ppb = MOBA_BLOCK // page
    assert n_pages % KSUM_CHUNK_PAGES == 0 and KSUM_CHUNK_PAGES % ppb == 0
    assert n_pages // ppb <= LANES and page % LANES == 0
    chunks_per_seq = n_pages // KSUM_CHUNK_PAGES
    cps = next(c for c in range(1, chunks_per_seq + 1)
               if chunks_per_seq % c == 0 and n_steps * c >= n_seq * chunks_per_seq)
    qmap = lambda b, p, qi, ki, pt: (b, qi[p], 0)
    kmap = lambda b, p, qi, ki, pt: (b, ki[p], 0)
    hp = MLA_HEADS * HEAD_PAD
    grid_spec = pltpu.PrefetchScalarGridSpec(
        num_scalar_prefetch=3, grid=(B, qi_tbl.shape[0]),
        in_specs=[pl.BlockSpec((1, t, hp), qmap), pl.BlockSpec((1, t, hp), kmap),
                  pl.BlockSpec((1, t, MLA_WIDTH), kmap), pl.BlockSpec((1, t, MLA_WIDTH), qmap),
                  pl.BlockSpec(memory_space=pl.ANY)],
        out_specs=[pl.BlockSpec((1, t, MLA_WIDTH), qmap), pl.BlockSpec(memory_space=pl.ANY)],
        scratch_shapes=[pltpu.VMEM((MLA_HEADS, t, LANES), F32), pltpu.VMEM((MLA_HEADS, t, LANES), F32),
                        pltpu.VMEM((MLA_HEADS, t, LANES), F32),
                        pltpu.VMEM((KSUM_STEP_RING, cps * KSUM_CHUNK_PAGES, H, Dh, page), F32),
                        pltpu.VMEM((H * Dh, LANES), F32),
                        pltpu.SemaphoreType.DMA((KSUM_STEP_RING,)), pltpu.SemaphoreType.DMA(())])
    return pl.pallas_call(
        functools.partial(_mla_attn_kernel, layer=layer, pages_per_block=ppb, chunks_per_step=cps),
        grid_spec=grid_spec,
        out_shape=[jax.ShapeDtypeStruct((B, S, MLA_WIDTH), BF16),
                   jax.ShapeDtypeStruct((n_seq, H * Dh, LANES), F32)],
        compiler_params=_params(("arbitrary", "arbitrary")), name="mla_attn_prompt",
    )(qi_tbl, ki_tbl, page_table, q, k, v, ga, cache_k_t)


def _cd_front(hp, g2, w2):
    h2 = _rms(hp, g2).astype(BF16)
    return _dot(h2, w2)


def _mid_kernel(a_ref, gm_ref, x_ref, wo_ref, g2_ref, w2_ref, cw_ref,
                hp_out, cg_out, q_out, kb_out, vb_out, mg_out, kf_out, vf_out, km_out, cp_out,
                pre_ref):
    tm = x_ref.shape[1]
    i = pl.program_id(1)
    w = SC_WIDTH
    half = wo_ref.shape[0] // 2
    hp = x_ref[0] + _dot(a_ref[0], wo_ref[:half, :]) + _dot(gm_ref[0], wo_ref[half:, :])
    hp_out[0] = hp
    z = _cd_front(hp, g2_ref[...], w2_ref[...])
    pre = z[:, 2 * w:3 * w] * z[:, :w]

    @pl.when(i == 0)
    def _():
        pre_ref[:SUBLANES, :] = jnp.zeros((SUBLANES, w), F32)

    pre_ref[SUBLANES:, :] = pre
    cw = cw_ref[...]
    conv = (cw[0:1] * pre_ref[SUBLANES - 2:SUBLANES - 2 + tm, :]
            + cw[1:2] * pre_ref[SUBLANES - 1:SUBLANES - 1 + tm, :] + cw[2:3] * pre)
    tail = pre_ref[tm:tm + SUBLANES, :]
    pre_ref[:SUBLANES, :] = tail
    cp_out[0] = tail[SUBLANES - (CONV_LEN - 1):, :]
    cg_out[0] = (z[:, w:2 * w] * conv * _silu(z[:, 3 * w:4 * w])).astype(BF16)
    q_out[0] = (z[:, 4 * w:5 * w] * (MOBA_SCALE * LOG2E)).astype(BF16)
    k = z[:, 5 * w:6 * w]
    v = z[:, 6 * w:7 * w]
    kf_out[0] = k
    vf_out[0] = v
    kb_out[0] = k.astype(BF16)
    vb_out[0] = v.astype(BF16)
    mg_out[0] = _silu(z[:, 7 * w:]).astype(BF16)
    for j in range(tm // MOBA_BLOCK):
        blk = k[j * MOBA_BLOCK:(j + 1) * MOBA_BLOCK]
        km_out[0, j] = jnp.sum(blk, axis=0, keepdims=True) * (1.0 / MOBA_BLOCK)


def _mid_prompt(a, gm, x, w_out, g2, w2, conv_w, tm):
    B, S, D = x.shape
    w = SC_WIDTH
    nb = S // MOBA_BLOCK
    tok = lambda width: pl.BlockSpec((1, tm, width), lambda b, i: (b, i, 0))
    in_specs = [tok(a.shape[2]), tok(gm.shape[2]), tok(D),
                _const_spec(w_out.shape), _const_spec((1, D)), _const_spec(w2.shape),
                _const_spec(conv_w.shape)]
    bf = lambda: jax.ShapeDtypeStruct((B, S, w), BF16)
    ff = lambda: jax.ShapeDtypeStruct((B, S, w), F32)
    out_shape = [jax.ShapeDtypeStruct((B, S, D), F32), bf(), bf(), bf(), bf(), bf(), ff(), ff(),
                 jax.ShapeDtypeStruct((B, nb, 1, w), F32),
                 jax.ShapeDtypeStruct((B, CONV_LEN - 1, w), F32)]
    bpt = tm // MOBA_BLOCK
    out_specs = [tok(D)] + [tok(w)] * 7 + [
        pl.BlockSpec((1, bpt, 1, w), lambda b, i: (b, i, 0, 0)),
        pl.BlockSpec((1, CONV_LEN - 1, w), lambda b, i: (b, 0, 0))]
    return pl.pallas_call(
        _mid_kernel, grid=(B, S // tm), in_specs=in_specs, out_specs=out_specs, out_shape=out_shape,
        scratch_shapes=[pltpu.VMEM((tm + SUBLANES, w), F32)],
        compiler_params=_params(("arbitrary", "arbitrary")), name="mid_prompt",
    )(a, gm, x, w_out, g2, w2, conv_w)


def _select_top_blocks_t(gate_t, blk, own):
    blk_f = blk.astype(F32)
    g = jnp.where(blk < own, gate_t, -jnp.inf)
    allowed = jnp.where(blk == own, 1.0, 0.0)
    for _ in range(MOBA_TOPK):
        mx = jnp.max(g, axis=0, keepdims=True)
        first = jnp.min(jnp.where(g == mx, blk_f, float(LANES)), axis=0, keepdims=True)
        pick = (blk_f == first) & (mx > -jnp.inf)
        allowed = jnp.where(pick, 1.0, allowed)
        g = jnp.where(pick, -jnp.inf, g)
    return allowed


def _moba_attn_kernel(qi_ref, kb_ref, q_ref, k_ref, v_ref, km_ref, mg_ref, o_ref,
                      qa_ref, m_ref, l_ref, acc_ref):
    tq, tk = q_ref.shape[1], k_ref.shape[1]
    q_blocks = tq // tk
    pair = pl.program_id(1)
    qi, kb = qi_ref[pair], kb_ref[pair]
    first_own = qi * q_blocks

    @pl.when(kb == 0)
    def _():
        _attn_init(m_ref, l_ref, acc_ref)
        lane = lax.broadcasted_iota(jnp.int32, (tq, LANES), 1)
        nbp = km_ref.shape[1]
        blk = lax.broadcasted_iota(jnp.int32, (nbp, tq), 0)
        qcol = lax.broadcasted_iota(jnp.int32, (nbp, tq), 1)
        assert tk & (tk - 1) == 0
        own = first_own + lax.shift_right_logical(qcol, tk.bit_length() - 1)
        eye = jnp.where(lax.broadcasted_iota(jnp.int32, (LANES, LANES), 0)
                        == lax.broadcasted_iota(jnp.int32, (LANES, LANES), 1), 1.0, 0.0).astype(BF16)
        for h in range(MOBA_HEADS):
            cols = slice((h // 2) * LANES, (h // 2 + 1) * LANES)
            in_head = (lane >= LANES // 2) if h % 2 else (lane < LANES // 2)
            qm = jnp.where(in_head, q_ref[0, :, cols].astype(F32), 0.0)
            gate_t = _dot_nt(km_ref[0, :, cols], qm, precision=lax.Precision.HIGHEST)
            allowed_t = _select_top_blocks_t(gate_t, blk, own)
            allowed_t = jnp.concatenate(
                [allowed_t, jnp.zeros((LANES - nbp, tq), F32)], axis=0).astype(BF16)
            allowed = jnp.concatenate(
                [_dot_nt(eye, allowed_t[:, c * LANES:(c + 1) * LANES]) for c in range(tq // LANES)],
                axis=0)
            bias = jnp.where(allowed > 0.5, 0.0, MASKED)
            qa_ref[h] = jnp.concatenate([qm.astype(BF16), bias.astype(BF16)], axis=1)

    def step(own_blocks):
        blk_lane = lax.broadcasted_iota(jnp.int32, (tk, LANES), 1)
        onehot = jnp.where(blk_lane == kb, 1.0, 0.0).astype(BF16)
        if own_blocks:
            qpos = qi * tq + lax.broadcasted_iota(jnp.int32, (tq, tk), 0)
            kpos = kb * tk + lax.broadcasted_iota(jnp.int32, (tq, tk), 1)
            keep = kpos <= qpos
        for h in range(MOBA_HEADS):
            cols = slice((h // 2) * LANES, (h // 2 + 1) * LANES)
            k_aug = jnp.concatenate([k_ref[0, :, cols], onehot], axis=1)
            s = _dot_nt(qa_ref[h], k_aug)
            if own_blocks:
                s = jnp.where(keep, s, MASKED)
            _softmax_step(s, _with_ones(v_ref[0, :, cols]), h, m_ref, l_ref, acc_ref)

    @pl.when(kb < first_own)
    def _():
        step(False)

    @pl.when(kb >= first_own)
    def _():
        step(True)

    @pl.when(kb == first_own + q_blocks - 1)
    def _():
        _attn_finish(mg_ref, o_ref, l_ref, acc_ref, MOBA_HEADS)


MOBA_Q_BLOCKS = 2


def _moba_attn_prompt(q, k, v, kmean_pad, mg):
    B, S, W = q.shape
    tk = MOBA_BLOCK
    tq = MOBA_Q_BLOCKS * tk
    assert S % tq == 0
    pairs = [(qi, kb) for qi in range(S // tq) for kb in range(MOBA_Q_BLOCKS * (qi + 1))]
    qi_tbl = jnp.asarray([p[0] for p in pairs], jnp.int32)
    kb_tbl = jnp.asarray([p[1] for p in pairs], jnp.int32)
    qmap = lambda b, p, qi, kb: (b, qi[p], 0)
    kmap = lambda b, p, qi, kb: (b, kb[p], 0)
    grid_spec = pltpu.PrefetchScalarGridSpec(
        num_scalar_prefetch=2, grid=(B, len(pairs)),
        in_specs=[pl.BlockSpec((1, tq, W), qmap), pl.BlockSpec((1, tk, W), kmap),
                  pl.BlockSpec((1, tk, W), kmap),
                  pl.BlockSpec((1, kmean_pad.shape[1], W), lambda b, p, qi, kb: (b, 0, 0)),
                  pl.BlockSpec((1, tq, W), qmap)],
        out_specs=pl.BlockSpec((1, tq, W), qmap),
        scratch_shapes=[pltpu.VMEM((MOBA_HEADS, tq, 2 * LANES), BF16)]
        + [pltpu.VMEM((MOBA_HEADS, tq, LANES), F32)] * 3)
    return pl.pallas_call(
        _moba_attn_kernel, grid_spec=grid_spec, out_shape=jax.ShapeDtypeStruct((B, S, W), BF16),
        compiler_params=_params(("arbitrary", "arbitrary")), name="moba_attn_prompt",
    )(qi_tbl, kb_tbl, q, k, v, kmean_pad, mg)


def _cd_out_kernel(cg_ref, a_ref, hp_ref, wo_ref, gf_ref, y_out):
    half = wo_ref.shape[0] // 2
    h = hp_ref[0] + _dot(cg_ref[0], wo_ref[:half, :]) + _dot(a_ref[0], wo_ref[half:, :])
    y_out[0] = _rms(h, gf_ref[...])


def _cd_out(cg, a, hp, w_out, gf, tm):
    B, S, D = hp.shape
    tok = lambda width: pl.BlockSpec((1, tm, width), lambda b, i: (b, i, 0))
    return pl.pallas_call(
        _cd_out_kernel, grid=(B, S // tm),
        in_specs=[tok(cg.shape[2]), tok(a.shape[2]), tok(D), _const_spec(w_out.shape),
                  _const_spec((1, D))],
        out_specs=tok(D), out_shape=jax.ShapeDtypeStruct((B, S, D), F32),
        compiler_params=_params(("arbitrary", "arbitrary")), name="cd_out",
    )(cg, a, hp, w_out, gf)


QD_WIDTH = 2 * LANES


def _prep_absorb(w_ukv):
    w_uk = w_ukv[..., :MLA_NOPE]
    to_lat = jnp.transpose(w_uk, (1, 2, 0))
    top = jnp.concatenate([to_lat, jnp.zeros((MLA_HEADS, MLA_NOPE, HEAD_PAD), w_ukv.dtype)], axis=2)
    eye = jnp.eye(HEAD_PAD, dtype=w_ukv.dtype)[MLA_NOPE:]
    bot = jnp.concatenate([jnp.zeros((HEAD_PAD - MLA_NOPE, MLA_KV_RANK), w_ukv.dtype), eye], axis=1)
    bot = jnp.broadcast_to(bot[None], (MLA_HEADS,) + bot.shape)
    return jnp.concatenate([top, bot], axis=1).astype(BF16)


def _ab_in_sample_kernel(x_ref, c_ref, g_ref, wa_ref, wb_ref, qg_ref, kvg_ref, wuq_ref, absorb_ref,
                         lng_ref, lnb_ref, w00_ref, b0_ref,
                         qd_out, ckv_out, kr128_out, kr_out, vln_out, ga_out, gmb_out):
    c = c_ref[...]
    q_raw, ckv, kr128, zb = _ab_front(x_ref[...], c, g_ref[...], wa_ref[...], wb_ref[...],
                                      qg_ref[...], kvg_ref[...], wuq_ref[...])
    ckv_out[...] = ckv
    kr128_out[...] = kr128
    kr_out[...] = kr128[:, MLA_NOPE:MLA_NOPE + MLA_ROPE]
    for h in range(MLA_HEADS):
        qh = _rope_mix(q_raw[:, h * HEAD_PAD:(h + 1) * HEAD_PAD], c[0], c[1]).astype(BF16)
        qd_out[:, h * QD_WIDTH:(h + 1) * QD_WIDTH] = _dot(qh, absorb_ref[h]).astype(BF16)
    w = GM_WIDTH
    ga_out[...] = _silu(zb[:, :w]).astype(BF16)
    v_ln = _layernorm(zb[:, 2 * w:3 * w], lng_ref[...], lnb_ref[...])
    vln_out[...] = v_ln
    s = v_ln * w00_ref[...] + b0_ref[...]
    gmb_out[...] = (zb[:, w:2 * w] * s * _silu(zb[:, 3 * w:])).astype(BF16)


def _ab_in_sample(x, tables, norm_g, w_a, w_b, qg, kvg, w_uq_pad, absorb, ln_g, ln_b, w00, b0):
    n = x.shape[0]
    out_shape = [
        jax.ShapeDtypeStruct((n, MLA_HEADS * QD_WIDTH), BF16),
        jax.ShapeDtypeStruct((n, MLA_KV_RANK), F32), jax.ShapeDtypeStruct((n, HEAD_PAD), F32),
        jax.ShapeDtypeStruct((n, MLA_ROPE), F32), jax.ShapeDtypeStruct((n, GM_WIDTH), F32),
        jax.ShapeDtypeStruct((n, MLA_WIDTH), BF16), jax.ShapeDtypeStruct((n, GM_WIDTH), BF16)]
    return pl.pallas_call(
        _ab_in_sample_kernel, out_shape=out_shape,
        compiler_params=pltpu.CompilerParams(vmem_limit_bytes=VMEM_LIMIT), name="ab_in_sample",
    )(x, tables, norm_g, w_a, w_b, qg, kvg, w_uq_pad, absorb, ln_g, ln_b, w00, b0)


MLA_DECODE_CHUNK_PAGES = 16


def _mla_decode_kernel(pt_ref, qd_ref, cnew_ref, rnew_ref, ckv_hbm, kr_hbm, o_ref,
                       ckv_buf, kr_buf, s_ref, sem, *, layer):
    b = pl.program_id(0)
    nb = pl.num_programs(0)
    n_pages, page = ckv_buf.shape[1], ckv_buf.shape[2]
    slot = b % 2

    def page_copies(seq, slot_, p):
        phys = pt_ref[seq, p]
        lanes = pl.ds(pl.multiple_of(p * page, page), page)
        return (pltpu.make_async_copy(ckv_hbm.at[layer, phys], ckv_buf.at[slot_, p], sem.at[0, slot_]),
                pltpu.make_async_copy(kr_hbm.at[layer, phys], kr_buf.at[slot_, :, lanes], sem.at[1, slot_]))

    def fetch(seq, slot_):
        def body(p, carry):
            for cp in page_copies(seq, slot_, p):
                cp.start()
            return carry
        lax.fori_loop(0, n_pages, body, 0, unroll=4)

    @pl.when(b == 0)
    def _():
        fetch(0, 0)

    @pl.when(b + 1 < nb)
    def _():
        fetch(b + 1, 1 - slot)

    def wait_body(p, carry):
        for cp in page_copies(b, slot, p):
            cp.wait()
        return carry
    lax.fori_loop(0, n_pages, wait_body, 0, unroll=4)

    qd = qd_ref[0]
    q_lat = qd[:, :MLA_KV_RANK]
    q_pad = qd[:, MLA_KV_RANK:]
    q_rope = q_pad[:, MLA_NOPE:MLA_NOPE + MLA_ROPE]
    cp_ = MLA_DECODE_CHUNK_PAGES
    rows = cp_ * page
    n_chunks = n_pages // cp_

    def load_latent(ci):
        pages = pl.ds(pl.multiple_of(ci * cp_, cp_), cp_)
        return ckv_buf[slot, pages].reshape(rows, MLA_KV_RANK).astype(BF16)

    def score_body(ci, m):
        pos = pl.ds(pl.multiple_of(ci * rows, rows), rows)
        r = kr_buf[slot, :, pos].astype(BF16)
        s = _dot_nt(q_lat, load_latent(ci)) + _dot(q_rope, r)
        s_ref[:, pos] = s
        return jnp.maximum(m, jnp.max(s, axis=1, keepdims=True))

    c_new = cnew_ref[0]
    s_new = (jnp.sum(q_lat.astype(F32) * c_new, axis=1, keepdims=True)
             + jnp.sum(q_pad.astype(F32) * rnew_ref[0], axis=1, keepdims=True))
    m = lax.fori_loop(0, n_chunks, score_body, s_new, unroll=True)

    def value_body(ci, carry):
        l, o = carry
        p = jnp.exp2(s_ref[:, pl.ds(pl.multiple_of(ci * rows, rows), rows)] - m)
        return l + jnp.sum(p, axis=1, keepdims=True), o + _dot(p.astype(BF16), load_latent(ci))

    p_new = jnp.exp2(s_new - m)
    l, o = lax.fori_loop(0, n_chunks, value_body, (p_new, p_new * c_new), unroll=True)
    o_ref[0] = o / l


def _mla_decode(page_table, qd, ckv_new, kr128_new, cache_ckv, cache_kr_t, layer):
    n, n_pages = page_table.shape
    page = cache_ckv.shape[2]
    assert n_pages % MLA_DECODE_CHUNK_PAGES == 0 and page % LANES == 0
    grid_spec = pltpu.PrefetchScalarGridSpec(
        num_scalar_prefetch=1, grid=(n,),
        in_specs=[pl.BlockSpec((1, MLA_HEADS, QD_WIDTH), lambda b, pt: (b, 0, 0)),
                  pl.BlockSpec((1, 1, MLA_KV_RANK), lambda b, pt: (b, 0, 0)),
                  pl.BlockSpec((1, 1, HEAD_PAD), lambda b, pt: (b, 0, 0)),
                  pl.BlockSpec(memory_space=pl.ANY), pl.BlockSpec(memory_space=pl.ANY)],
        out_specs=pl.BlockSpec((1, MLA_HEADS, MLA_KV_RANK), lambda b, pt: (b, 0, 0)),
        scratch_shapes=[pltpu.VMEM((2, n_pages, page, MLA_KV_RANK), F32),
                        pltpu.VMEM((2, MLA_ROPE, n_pages * page), F32),
                        pltpu.VMEM((MLA_HEADS, n_pages * page), F32),
                        pltpu.SemaphoreType.DMA((2, 2))])
    return pl.pallas_call(
        functools.partial(_mla_decode_kernel, layer=layer), grid_spec=grid_spec,
        out_shape=jax.ShapeDtypeStruct((n, MLA_HEADS, MLA_KV_RANK), F32),
        compiler_params=_params(("arbitrary",)), name="mla_decode",
    )(page_table, qd.reshape(n, MLA_HEADS, QD_WIDTH), ckv_new.reshape(n, 1, MLA_KV_RANK),
      kr128_new.reshape(n, 1, HEAD_PAD), cache_ckv, cache_kr_t)


def _mid_sample_kernel(ol_ref, wuv_ref, ga_ref, gm_ref, x_ref, wo_ref, g2_ref, w2_ref, cw_ref,
                       s0_ref, s1_ref,
                       hp_out, cg_out, q_out, k_out, v_out, mg_out, pre_out):
    w = SC_WIDTH
    half = wo_ref.shape[0] // 2
    att = _dot(ol_ref[...].astype(BF16), wuv_ref[...])
    a = (att * ga_ref[...].astype(F32)).astype(BF16)
    hp = x_ref[...] + _dot(a, wo_ref[:half, :]) + _dot(gm_ref[...], wo_ref[half:, :])
    hp_out[...] = hp
    z = _cd_front(hp, g2_ref[...], w2_ref[...])
    pre = z[:, 2 * w:3 * w] * z[:, :w]
    pre_out[...] = pre
    cw = cw_ref[...]
    conv = cw[0:1] * s0_ref[...] + cw[1:2] * s1_ref[...] + cw[2:3] * pre
    cg_out[...] = (z[:, w:2 * w] * conv * _silu(z[:, 3 * w:4 * w])).astype(BF16)
    q_out[...] = z[:, 4 * w:5 * w] * MOBA_SCALE
    k_out[...] = z[:, 5 * w:6 * w]
    v_out[...] = z[:, 6 * w:7 * w]
    mg_out[...] = _silu(z[:, 7 * w:])


def _mid_sample(o_lat, w_uv_bd, ga, gm, x, w_out, g2, w2, conv_w, s0, s1):
    n, D = x.shape
    w = SC_WIDTH
    ff = lambda: jax.ShapeDtypeStruct((n, w), F32)
    bf = lambda: jax.ShapeDtypeStruct((n, w), BF16)
    out_shape = [jax.ShapeDtypeStruct((n, D), F32), bf(), ff(), ff(), ff(), ff(), ff()]
    return pl.pallas_call(
        _mid_sample_kernel, out_shape=out_shape,
        compiler_params=pltpu.CompilerParams(vmem_limit_bytes=VMEM_LIMIT), name="mid_sample",
    )(o_lat, w_uv_bd, ga, gm, x, w_out, g2, w2, conv_w, s0, s1)


SELECT_GROUP = 8


def _moba_select_kernel(ks_ref, q_ref, idx_out, *, n_blocks):
    G, H = idx_out.shape[0], idx_out.shape[1]
    W = q_ref.shape[2]
    row = lax.broadcasted_iota(jnp.int32, (H, W), 0)
    col = lax.broadcasted_iota(jnp.int32, (H, W), 1)
    dh = W // H
    assert dh & (dh - 1) == 0
    in_head = lax.shift_right_logical(col, dh.bit_length() - 1) == row
    lane = lax.broadcasted_iota(jnp.int32, (H, LANES), 1)
    for i in range(G):
        q_heads = jnp.where(in_head, q_ref[i], 0.0)
        gate = jnp.dot(q_heads, ks_ref[i], precision=lax.Precision.HIGHEST,
                       preferred_element_type=F32)
        gate = jnp.where(lane < n_blocks, gate, -jnp.inf)
        idx = jnp.zeros(gate.shape, jnp.int32)
        for t in range(MOBA_TOPK):
            mx = jnp.max(gate, axis=1, keepdims=True)
            first = jnp.min(jnp.where(gate == mx, lane, LANES), axis=1, keepdims=True)
            idx = jnp.where(lane == t, first, idx)
            gate = jnp.where(lane == first, -jnp.inf, gate)
        idx_out[i] = idx


def _moba_select(ksum, q, n_blocks):
    n, W, _ = ksum.shape
    g = max(d for d in range(1, SELECT_GROUP + 1) if n % d == 0)
    return pl.pallas_call(
        functools.partial(_moba_select_kernel, n_blocks=n_blocks), grid=(n // g,),
        in_specs=[pl.BlockSpec((g, W, LANES), lambda b: (b, 0, 0)),
                  pl.BlockSpec((g, 1, W), lambda b: (b, 0, 0))],
        out_specs=pl.BlockSpec((g, MOBA_HEADS, LANES), lambda b: (b, 0, 0)),
        out_shape=jax.ShapeDtypeStruct((n, MOBA_HEADS, LANES), jnp.int32),
        compiler_params=_params(("arbitrary",)), name="moba_select",
    )(ksum, q)


def _moba_decode_kernel(pt_ref, idx_ref, q_ref, kn_ref, vn_ref, mg_ref, k_hbm, v_hbm, o_ref,
                        kbuf, vbuf, sem, *, layer, pages_per_block):
    b = pl.program_id(0)
    nb = pl.num_programs(0)
    H = q_ref.shape[1]
    slot = b % 2
    page = kbuf.shape[3] // (MOBA_TOPK * pages_per_block)

    def copies(seq, slot_):
        out = []
        for h in range(H):
            for t in range(MOBA_TOPK):
                blk = idx_ref[seq, h * MOBA_TOPK + t]
                for r in range(pages_per_block):
                    phys = pt_ref[seq, blk * pages_per_block + r]
                    lanes = pl.ds((t * pages_per_block + r) * page, page)
                    out.append(pltpu.make_async_copy(k_hbm.at[layer, phys, h],
                                                     kbuf.at[slot_, h, :, lanes], sem.at[0, slot_]))
                    out.append(pltpu.make_async_copy(v_hbm.at[layer, phys, h],
                                                     vbuf.at[slot_, h, :, lanes], sem.at[1, slot_]))
        return out

    @pl.when(b == 0)
    def _():
        for cp in copies(0, 0):
            cp.start()

    @pl.when(b + 1 < nb)
    def _():
        for cp in copies(b + 1, 1 - slot):
            cp.start()

    for cp in copies(b, slot):
        cp.wait()

    q = q_ref[0]
    qb = q.astype(BF16)
    k_new, v_new = kn_ref[0], vn_ref[0]
    s_new = jnp.sum(q * k_new, axis=1, keepdims=True)
    row = lax.broadcasted_iota(jnp.int32, q.shape, 0)
    out = jnp.zeros(q.shape, F32)
    for h in range(H):
        kh = kbuf[slot, h].astype(BF16)
        vh = vbuf[slot, h].astype(BF16)
        s = _dot(qb, kh)
        m = jnp.maximum(jnp.max(s, axis=1, keepdims=True), s_new)
        p = jnp.exp(s - m)
        p_new = jnp.exp(s_new - m)
        l = jnp.sum(p, axis=1, keepdims=True) + p_new
        o = (_dot_nt(p.astype(BF16), vh) + p_new * v_new) / l
        out = jnp.where(row == h, o, out)
    o_ref[0] = out * mg_ref[0]


def _moba_decode(page_table, idx, q, k_new, v_new, mg, cache_k_t, cache_v_t, layer):
    n, n_pages = page_table.shape
    _, _, H, Dh, page = cache_k_t.shape
    ppb = MOBA_BLOCK // page
    tok = pl.BlockSpec((1, H, Dh), lambda b, pt, ix: (b, 0, 0))
    grid_spec = pltpu.PrefetchScalarGridSpec(
        num_scalar_prefetch=2, grid=(n,),
        in_specs=[tok, tok, tok, tok, pl.BlockSpec(memory_space=pl.ANY),
                  pl.BlockSpec(memory_space=pl.ANY)],
        out_specs=tok,
        scratch_shapes=[pltpu.VMEM((2, H, Dh, MOBA_TOPK * ppb * page), F32),
                        pltpu.VMEM((2, H, Dh, MOBA_TOPK * ppb * page), F32),
                        pltpu.SemaphoreType.DMA((2, 2))])
    return pl.pallas_call(
        functools.partial(_moba_decode_kernel, layer=layer, pages_per_block=ppb),
        grid_spec=grid_spec, out_shape=jax.ShapeDtypeStruct((n, H, Dh), F32),
        compiler_params=_params(("arbitrary",)), name="moba_decode",
    )(page_table, idx, q, k_new, v_new, mg, cache_k_t, cache_v_t)


PROJ_TILE = 512
ATTN_TILE = 512


def kernel(x_prompt, x_sample, cache_mla_ckv, cache_mla_krope, cache_moba_k, cache_moba_v, state_conv,
           page_table, ab_norm_g, ab_w_in, ab_q_norm_g, ab_kv_norm_g, ab_w_uq, ab_w_ukv, ab_gm_ln_g,
           ab_gm_ln_b, ab_gm_ws, ab_gm_bs, ab_w_out, cd_norm_g, cd_w_in, cd_conv_w, cd_w_out,
           final_norm_g):
    B, S, D = x_prompt.shape
    DB, T, _ = x_sample.shape
    n_pages = page_table.shape[1]
    page = cache_mla_ckv.shape[2]
    past_len = n_pages * page
    assert T == 1 and ab_w_in.shape[0] == 1 and cd_w_in.shape[0] == 1
    assert past_len % MOBA_BLOCK == 0 and past_len // MOBA_BLOCK >= MOBA_TOPK
    assert S % PROJ_TILE == 0 and S % ATTN_TILE == 0 and PROJ_TILE % MOBA_BLOCK == 0
    li = 0
    row = lambda v: v.reshape(1, -1)

    w_a, w_b, w_uq_pad, w_uk_pad, w_uv_flat = _prep_ab_weights(ab_w_in[li], ab_w_uq[li], ab_w_ukv[li])
    absorb = _prep_absorb(ab_w_ukv[li])
    w_uv = ab_w_ukv[li][..., MLA_NOPE:]
    w_uv_bd = (jnp.eye(MLA_HEADS, dtype=F32)[:, None, :, None] * jnp.transpose(w_uv, (1, 0, 2))[:, :, None, :])
    w_uv_bd = w_uv_bd.reshape(MLA_HEADS * MLA_KV_RANK, MLA_WIDTH).astype(BF16)
    ws = ab_gm_ws[li]
    bs_exp = jnp.repeat(ab_gm_bs[li].T, GM_GROUP_DIM, axis=1)
    w00 = jnp.repeat(ws[:, 0, 0], GM_GROUP_DIM).reshape(1, GM_WIDTH)
    b0 = bs_exp[0:1]
    w_out0 = ab_w_out[li].astype(BF16)
    w2 = cd_w_in[li].astype(BF16)
    w_out1 = cd_w_out[li].astype(BF16)
    conv_w = cd_conv_w[li]
    tab_p = _rope_tables(jnp.arange(S, dtype=jnp.int32))
    tab_s = _rope_tables(jnp.full((DB,), past_len, jnp.int32))
    g0, qg, kvg = row(ab_norm_g[li]), row(ab_q_norm_g[li]), row(ab_kv_norm_g[li])
    ln_g, ln_b = row(ab_gm_ln_g[li]), row(ab_gm_ln_b[li])
    g2, gf = row(cd_norm_g[li]), row(final_norm_g)

    q, k, v, ckv_p, kr_p, ga, gmb = _ab_in_prompt(
        x_prompt, tab_p, g0, w_a, w_b, qg, kvg, w_uq_pad, w_uk_pad, w_uv_flat, ln_g, ln_b, ws, bs_exp,
        PROJ_TILE)
    cache_kr_t = jnp.transpose(cache_mla_krope, (0, 1, 3, 2))
    cache_k_t = jnp.transpose(cache_moba_k, (0, 1, 3, 4, 2))
    cache_v_t = jnp.transpose(cache_moba_v, (0, 1, 3, 4, 2))
    a0, ksum = _mla_attn_prompt(q, k, v, ga, ATTN_TILE, page_table, cache_k_t, li)
    hp, cg, q1, k1b, v1b, mg, k1, v1, kmean, conv_p = _mid_prompt(
        a0, gmb, x_prompt, w_out0, g2, w2, conv_w, PROJ_TILE)
    nb = S // MOBA_BLOCK
    assert nb <= LANES
    nb_pad = -(-nb // SUBLANES) * SUBLANES
    kmean_pad = jnp.pad(kmean.reshape(B, nb, MOBA_WIDTH), ((0, 0), (0, nb_pad - nb), (0, 0)))
    a1 = _moba_attn_prompt(q1, k1b, v1b, kmean_pad, mg)
    y_prompt = _cd_out(cg, a1, hp, w_out1, gf, PROJ_TILE)

    xs = x_sample.reshape(DB, D)
    qd, ckv_s, kr128_s, kr_s, vln_s, ga_s, gmb_s = _ab_in_sample(
        xs, tab_s, g0, w_a, w_b, qg, kvg, w_uq_pad, absorb, ln_g, ln_b, w00, b0)
    o_lat = _mla_decode(page_table, qd, ckv_s, kr128_s, cache_mla_ckv, cache_kr_t, li)
    hs, cg_s, q_s, k_s, v_s, mg_s, pre_s = _mid_sample(
        o_lat.reshape(DB, MLA_HEADS * MLA_KV_RANK), w_uv_bd, ga_s, gmb_s, xs, w_out0, g2, w2, conv_w,
        state_conv[li, :, 0], state_conv[li, :, 1])
    heads = lambda t: t.reshape(DB, MOBA_HEADS, MOBA_HEAD_DIM)
    idx = _moba_select(ksum, q_s.reshape(DB, 1, MOBA_WIDTH), past_len // MOBA_BLOCK)
    idx = idx[:, :, :MOBA_TOPK].reshape(DB, MOBA_HEADS * MOBA_TOPK)
    a1_s = _moba_decode(page_table, idx, heads(q_s), heads(k_s), heads(v_s),
                        heads(mg_s), cache_k_t, cache_v_t, li)
    y_sample = _cd_out(cg_s[None], a1_s.reshape(1, DB, MOBA_WIDTH).astype(BF16), hs[None], w_out1, gf, DB)

    return (y_prompt, y_sample.reshape(DB, 1, D),
            ckv_p[None], kr_p[None],
            ckv_s.reshape(1, DB, 1, MLA_KV_RANK), kr_s.reshape(1, DB, 1, MLA_ROPE),
            vln_s.reshape(1, DB, 1, GM_WIDTH),
            conv_p[None], jnp.stack([state_conv[li, :, 1], pre_s], axis=1)[None],
            k1.reshape(1, B, S, MOBA_HEADS, MOBA_HEAD_DIM), v1.reshape(1, B, S, MOBA_HEADS, MOBA_HEAD_DIM),
            k_s.reshape(1, DB, 1, MOBA_HEADS, MOBA_HEAD_DIM), v_s.reshape(1, DB, 1, MOBA_HEADS, MOBA_HEAD_DIM))
```
